```python
import math
import jax
import jax.numpy as jnp
from jax import lax
import numpy as np

D_MODEL = 2048
BATCH = 2
SEQ = 8192
DEPTH = 1

GRID_W = 64
CTX_LEN = 256
EPS = 1e-6

M_HEADS = 8
M_HEAD_DIM = 128
M_WIDTH = M_HEADS * M_HEAD_DIM
M_CHUNK = 128
CONV_K = 5
F_BIAS_LO = 3.0
F_BIAS_HI = 6.0

A_HEADS = 8
A_HALF_DIM = 64
A_V_DIM = 2 * A_HALF_DIM
A_WIDTH = A_HEADS * A_V_DIM
Q_BLOCK = 128
ROPE_BASE = 10000.0

P_HEADS = 8
N_KEYS = 128
N_EXPERTS = N_KEYS * N_KEYS
P_KEY_DIM = 256
P_TOPK = 16
TOKEN_BLOCK = 128

IN_SIZES = (M_WIDTH, M_WIDTH, M_WIDTH, M_WIDTH, 4 * M_HEADS, A_WIDTH, A_WIDTH, A_WIDTH, D_MODEL, D_MODEL)
P_IN = sum(IN_SIZES)
GATE_OFFSET = 4 * M_WIDTH

kernel_name = 'hybrid_mlstm_diffattn_peer_dit_layer'


def rmsnorm(x, w):
    xf = x.astype(jnp.float32)
    y = xf * lax.rsqrt(jnp.mean(xf * xf, axis=-1, keepdims=True) + EPS)
    return (y * w.astype(jnp.float32)).astype(x.dtype)


def modulate(h, shift, scale):
    return h * (1 + scale) + shift


def split_proj(p):
    return jnp.split(p, np.cumsum(IN_SIZES)[:-1].tolist(), axis=-1)


def to_heads(a, n_heads):
    B, T, _ = a.shape
    return a.reshape(B, T, n_heads, -1).transpose(0, 2, 1, 3)


def axial_rope(n_tokens):
    rows = n_tokens // GRID_W
    row = jnp.repeat(jnp.arange(rows, dtype=jnp.float32), GRID_W)
    col = jnp.tile(jnp.arange(GRID_W, dtype=jnp.float32), rows)
    axis_dim = A_HALF_DIM // 2
    inv_freq = ROPE_BASE ** (-jnp.arange(0, axis_dim, 2, dtype=jnp.float32) / axis_dim)
    ang = jnp.concatenate([row[:, None] * inv_freq, col[:, None] * inv_freq], axis=-1)
    return jnp.cos(ang), jnp.sin(ang)


def apply_rope(x, cos, sin):
    x1, x2 = jnp.split(x, 2, axis=-1)
    cos = cos.astype(x.dtype)
    sin = sin.astype(x.dtype)
    return jnp.concatenate([x1 * cos - x2 * sin, x2 * cos + x1 * sin], axis=-1)


def dwconv_centred(x, w, b):
    C = x.shape[-1]
    y = lax.conv_general_dilated(x, w[:, None, :], window_strides=(1,),
                                 padding=[(CONV_K // 2, CONV_K // 2)],
                                 dimension_numbers=('NWC', 'WIO', 'NWC'),
                                 feature_group_count=C)
    return y + b


def mlstm_prep(qm, km, vm, gm, conv_w, conv_b):
    qk = jax.nn.silu(dwconv_centred(jnp.concatenate([qm, km], axis=-1), conv_w, conv_b))
    qm, km = jnp.split(qk, 2, axis=-1)
    q = to_heads(qm, M_HEADS).astype(jnp.float32) * (M_HEAD_DIM ** -0.5)
    k = to_heads(km, M_HEADS).astype(jnp.float32)
    v = to_heads(vm, M_HEADS).astype(jnp.float32)
    B, T, _ = gm.shape
    g = gm.astype(jnp.float32).reshape(B, T, 4, M_HEADS).transpose(2, 0, 3, 1)
    fwd = (g[0], jax.nn.log_sigmoid(g[1]))
    bwd = (g[2], jax.nn.log_sigmoid(g[3]))
    return q, k, v, fwd, bwd


def mlstm_scan(q, k, v, i_pre, log_f, state, with_output):
    B, H, T, d = q.shape
    nc = T // M_CHUNK

    def chunks(a):
        return jnp.moveaxis(a.reshape(B, H, nc, M_CHUNK, *a.shape[3:]), 2, 0)

    tri = jnp.tril(jnp.ones((M_CHUNK, M_CHUNK), dtype=bool))

    def step(carry, xs):
        C, n, m = carry
        qc, kc, vc, ic, fc = xs
        b = jnp.cumsum(fc, axis=-1)
        b_end = b[..., -1]
        log_w_end = b_end[..., None] - b + ic
        m_end = jnp.maximum(b_end + m, jnp.max(log_w_end, axis=-1))
        w_end = jnp.exp(log_w_end - m_end[..., None])
        decay = jnp.exp(b_end + m - m_end)
        C_new = decay[..., None, None] * C + jnp.einsum('bhsk,bhsv->bhkv', kc * w_end[..., None], vc)
        n_new = decay[..., None] * n + jnp.einsum('bhs,bhsk->bhk', w_end, kc)
        out = None
        if with_output:
            log_d = jnp.where(tri, b[..., :, None] - b[..., None, :] + ic[..., None, :], -jnp.inf)
            m_t = jnp.maximum(b + m[..., None], jnp.max(log_d, axis=-1))
            inter = jnp.exp(b + m[..., None] - m_t)
            qk = jnp.einsum('bhtk,bhsk->bhts', qc, kc) * jnp.exp(log_d - m_t[..., None])
            num = inter[..., None] * jnp.einsum('bhtk,bhkv->bhtv', qc, C) + jnp.einsum('bhts,bhsv->bhtv', qk, vc)
            den = inter * jnp.einsum('bhtk,bhk->bht', qc, n) + jnp.sum(qk, axis=-1)
            out = num / jnp.maximum(jnp.abs(den), jnp.exp(-m_t))[..., None]
        return (C_new, n_new, m_end), out

    state, h = lax.scan(step, state, tuple(chunks(a) for a in (q, k, v, i_pre, log_f)))
    if with_output:
        h = jnp.moveaxis(h, 0, 2).reshape(B, H, T, d)
    return h, state


def mlstm_bidirectional(lat, ctx_s, ctx_out):
    q, k, v, gf, gb = lat
    qc, kc, vc, gfc, gbc = ctx_s
    B, H = q.shape[:2]
    init = (jnp.zeros((B, H, M_HEAD_DIM, M_HEAD_DIM), jnp.float32),
            jnp.zeros((B, H, M_HEAD_DIM), jnp.float32),
            jnp.zeros((B, H), jnp.float32))

    def flip(a):
        return jnp.flip(a, axis=2)

    hc_f, st_f = mlstm_scan(qc, kc, vc, gfc[0], gfc[1], init, ctx_out)
    h_f, _ = mlstm_scan(q, k, v, gf[0], gf[1], st_f, True)
    hc_b, st_b = mlstm_scan(flip(qc), flip(kc), flip(vc), flip(gbc[0]), flip(gbc[1]), init, ctx_out)
    h_b, _ = mlstm_scan(flip(q), flip(k), flip(v), flip(gb[0]), flip(gb[1]), st_b, True)
    h_lat = h_f + flip(h_b)
    h_ctx = hc_f + flip(hc_b) if ctx_out else None
    return h_lat, h_ctx


def mlstm_out(h, z, norm_w):
    B, H, T, d = h.shape
    hn = rmsnorm(h.transpose(0, 2, 1, 3), norm_w).reshape(B, T, H * d)
    return (jax.nn.sigmoid(z.astype(jnp.float32)) * hn).astype(z.dtype)


def diff_prep(qa, ka, va):
    B, T, _ = qa.shape
    q = qa.reshape(B, T, A_HEADS, 2, A_HALF_DIM).transpose(0, 2, 3, 1, 4)
    k = ka.reshape(B, T, A_HEADS, 2, A_HALF_DIM).transpose(0, 2, 3, 1, 4)
    v = to_heads(va, A_HEADS)
    return q, k, v


def diff_attention(q, k, v, lam):
    B, H, _, Tq, d = q.shape
    nb = Tq // Q_BLOCK
    qb = jnp.moveaxis(q.reshape(B, H, 2, nb, Q_BLOCK, d), 3, 0)
    scale = A_HALF_DIM ** -0.5

    def block(qx):
        s = jnp.einsum('bhmqd,bhmkd->bhmqk', qx, k).astype(jnp.float32) * scale
        p = jax.nn.softmax(s, axis=-1)
        a = p[:, :, 0] - lam * p[:, :, 1]
        return jnp.einsum('bhqk,bhkv->bhqv', a.astype(v.dtype), v)

    o = lax.map(block, qb)
    return jnp.moveaxis(o, 0, 2).reshape(B, H, Tq, v.shape[-1])


def diff_out(o, norm_w, lam_init):
    B, H, T, dv = o.shape
    return (rmsnorm(o.transpose(0, 2, 1, 3), norm_w) * (1 - lam_init)).reshape(B, T, H * dv)


def merge_branches(y_m, y_d, ga, gb, w_pa, w_pb, w_out):
    return (jax.nn.sigmoid(ga) * (y_m @ w_pa) + jax.nn.sigmoid(gb) * (y_d @ w_pb)) @ w_out


def token_mixer(h, hc, cos, sin, lam, lam_init, w_in, b_in, conv_w, conv_b, m_norm_w,
                a_norm_w, w_pa, w_pb, w_out, ctx_out):
    qm, km, vm, zm, gm, qa, ka, va, ga, gb = split_proj(h @ w_in + b_in)
    qmc, kmc, vmc, zmc, gmc, qac, kac, vac, gac, gbc = split_proj(hc @ w_in + b_in)
    hm, hmc = mlstm_bidirectional(mlstm_prep(qm, km, vm, gm, conv_w, conv_b),
                                  mlstm_prep(qmc, kmc, vmc, gmc, conv_w, conv_b), ctx_out)
    q, k, v = diff_prep(qa, ka, va)
    q = apply_rope(q, cos, sin)
    k = apply_rope(k, cos, sin)
    qc, kc, vc = diff_prep(qac, kac, vac)
    o = diff_attention(q, jnp.concatenate([k, kc], axis=3), jnp.concatenate([v, vc], axis=2), lam)
    y = merge_branches(mlstm_out(hm, zm, m_norm_w), diff_out(o, a_norm_w, lam_init), ga, gb, w_pa, w_pb, w_out)
    if not ctx_out:
        return y, None
    oc = diff_attention(qc, kc, vc, lam)
    yc = merge_branches(mlstm_out(hmc, zmc, m_norm_w), diff_out(oc, a_norm_w, lam_init), gac, gbc, w_pa, w_pb, w_out)
    return y, yc


def peer(h, w_pq, sub_keys, expert_u, expert_v):
    B, T, D = h.shape
    q = (h @ w_pq).reshape(B, T, P_HEADS, 2, P_KEY_DIM // 2)
    s = jnp.einsum('bthpd,hpnd->bthpn', q, sub_keys).astype(jnp.float32)
    s1, i1 = lax.top_k(s[..., 0, :], P_TOPK)
    s2, i2 = lax.top_k(s[..., 1, :], P_TOPK)
    cand_s = (s1[..., :, None] + s2[..., None, :]).reshape(B, T, P_HEADS, P_TOPK * P_TOPK)
    cand_i = (i1[..., :, None] * N_KEYS + i2[..., None, :]).reshape(B, T, P_HEADS, P_TOPK * P_TOPK)
    top_s, pos = lax.top_k(cand_s, P_TOPK)
    idx = jnp.take_along_axis(cand_i, pos, axis=-1)
    g = jax.nn.softmax(top_s, axis=-1)
    nb = (B * T) // TOKEN_BLOCK
    E = P_HEADS * P_TOPK

    def block(args):
        hx, ix, gx = args
        act = jax.nn.gelu(jnp.einsum('td,ted->te', hx, expert_u[ix]).astype(jnp.float32), approximate=False)
        return jnp.einsum('te,ted->td', (gx * act).astype(h.dtype), expert_v[ix])

    out = lax.map(block, (h.reshape(nb, TOKEN_BLOCK, D), idx.reshape(nb, TOKEN_BLOCK, E),
                          g.reshape(nb, TOKEN_BLOCK, E)))
    return out.reshape(B, T, D)


def setup_inputs(seed: int = 0) -> dict:
    key = jax.random.key(seed)
    ks = jax.random.split(key, 24)
    f32 = jnp.float32
    D = D_MODEL

    def nrm(k, shape, scale):
        return jax.random.normal(k, shape, f32) * scale

    f_bias = np.zeros((P_IN,), np.float32)
    f_lin = np.linspace(F_BIAS_LO, F_BIAS_HI, M_HEADS, dtype=np.float32)
    f_bias[GATE_OFFSET + M_HEADS:GATE_OFFSET + 2 * M_HEADS] = f_lin
    f_bias[GATE_OFFSET + 3 * M_HEADS:GATE_OFFSET + 4 * M_HEADS] = f_lin
    return {
        'x': nrm(ks[0], (BATCH, SEQ, D), 1.0),
        'c': nrm(ks[1], (BATCH, D), 1.0),
        'ctx': nrm(ks[2], (BATCH, CTX_LEN, D), 1.0),
        'c_ctx': nrm(ks[3], (D,), 1.0),
        'w_mod': nrm(ks[4], (DEPTH, D, 6 * D), 0.5 * D ** -0.5),
        'b_mod': nrm(ks[5], (DEPTH, 6 * D), 0.02),
        'norm1_w': 1.0 + nrm(ks[6], (DEPTH, D), 0.02),
        'w_in': nrm(ks[7], (DEPTH, D, P_IN), D ** -0.5),
        'b_in': nrm(ks[8], (DEPTH, P_IN), 0.02) + jnp.asarray(f_bias),
        'conv_w': nrm(ks[9], (DEPTH, CONV_K, 2 * M_WIDTH), CONV_K ** -0.5),
        'conv_b': nrm(ks[10], (DEPTH, 2 * M_WIDTH), 0.02),
        'm_norm_w': 1.0 + nrm(ks[11], (DEPTH, M_HEADS, M_HEAD_DIM), 0.02),
        'lambdas': nrm(ks[12], (DEPTH, 4, A_HALF_DIM), 0.1),
        'a_norm_w': 1.0 + nrm(ks[13], (DEPTH, A_V_DIM), 0.02),
        'w_pa': nrm(ks[14], (DEPTH, M_WIDTH, D), M_WIDTH ** -0.5),
        'w_pb': nrm(ks[15], (DEPTH, A_WIDTH, D), A_WIDTH ** -0.5),
        'w_out': nrm(ks[16], (DEPTH, D, D), D ** -0.5),
        'norm2_w': 1.0 + nrm(ks[17], (DEPTH, D), 0.02),
        'w_pq': nrm(ks[18], (DEPTH, D, P_HEADS * P_KEY_DIM), D ** -0.5),
        'sub_keys': nrm(ks[19], (DEPTH, P_HEADS, 2, N_KEYS, P_KEY_DIM // 2), (P_KEY_DIM // 2) ** -0.5),
        'expert_u': nrm(ks[20], (DEPTH, N_EXPERTS, D), D ** -0.5),
        'expert_v': nrm(ks[21], (DEPTH, N_EXPERTS, D), (P_HEADS * P_TOPK) ** -0.5),
        'final_norm_w': 1.0 + nrm(ks[22], (D,), 0.02),
    }


def reference(x, c, ctx, c_ctx, w_mod, b_mod, norm1_w, w_in, b_in, conv_w, conv_b, m_norm_w,
              lambdas, a_norm_w, w_pa, w_pb, w_out, norm2_w, w_pq, sub_keys, expert_u, expert_v,
              final_norm_w):
    cos, sin = axial_rope(x.shape[1])
    for l in range(DEPTH):
        ctx_out = l < DEPTH - 1
        lam_init = 0.8 - 0.6 * math.exp(-0.3 * l)
        lq1, lk1, lq2, lk2 = lambdas[l].astype(jnp.float32)
        lam = jnp.exp(jnp.sum(lq1 * lk1)) - jnp.exp(jnp.sum(lq2 * lk2)) + lam_init
        mod = jax.nn.silu(c) @ w_mod[l] + b_mod[l]
        sh1, sc1, g1, sh2, sc2, g2 = jnp.split(mod[:, None, :], 6, axis=-1)
        mod_c = jax.nn.silu(c_ctx) @ w_mod[l] + b_mod[l]
        csh1, csc1, cg1, csh2, csc2, cg2 = jnp.split(mod_c, 6, axis=-1)
        h = modulate(rmsnorm(x, norm1_w[l]), sh1, sc1)
        hc = modulate(rmsnorm(ctx, norm1_w[l]), csh1, csc1)
        y, yc = token_mixer(h, hc, cos, sin, lam, lam_init, w_in[l], b_in[l], conv_w[l], conv_b[l],
                            m_norm_w[l], a_norm_w[l], w_pa[l], w_pb[l], w_out[l], ctx_out)
        x = x + g1 * y
        x = x + g2 * peer(modulate(rmsnorm(x, norm2_w[l]), sh2, sc2), w_pq[l], sub_keys[l], expert_u[l], expert_v[l])
        if ctx_out:
            ctx = ctx + cg1 * yc
            ctx = ctx + cg2 * peer(modulate(rmsnorm(ctx, norm2_w[l]), csh2, csc2),
                                   w_pq[l], sub_keys[l], expert_u[l], expert_v[l])
    return rmsnorm(x, final_norm_w)
```

```python
import functools
import math

import jax
import jax.numpy as jnp
import numpy as np
from jax import lax
from jax.experimental import pallas as pl
from jax.experimental.pallas import tpu as pltpu

F32 = jnp.float32
BF16 = jnp.bfloat16

EPS = 1e-6
GRID_W = 64
ROPE_BASE = 10000.0
M_HEADS = 8
M_HEAD_DIM = 128
M_CHUNK = 128
CONV_K = 5
A_HEADS = 8
A_HALF_DIM = 64
P_HEADS = 8
P_TOPK = 16
LANES = 128
SUBLANES = 8
VMEM_LIMIT = 56 * 1024 * 1024

NT_DIMS = (((1,), (1,)), ((), ()))
TN_DIMS = (((0,), (0,)), ((), ()))


def _params(*sem):
    return pltpu.CompilerParams(dimension_semantics=sem, vmem_limit_bytes=VMEM_LIMIT)


def _sigmoid(x):
    return 1.0 / (1.0 + jnp.exp(-x))


def _mod_kernel(c_ref, w_ref, b_ref, o_ref):
    c = c_ref[...]
    s = (c * _sigmoid(c)).astype(BF16)
    o_ref[...] = jnp.dot(s, w_ref[...].astype(BF16), preferred_element_type=F32) + b_ref[...]


def _mod(cvec, w_mod, b_mod):
    rows, d = cvec.shape
    n = w_mod.shape[1]
    tn = 1024
    return pl.pallas_call(
        _mod_kernel,
        grid=(n // tn,),
        in_specs=[pl.BlockSpec((rows, d), lambda j: (0, 0)),
                  pl.BlockSpec((d, tn), lambda j: (0, j)),
                  pl.BlockSpec((1, tn), lambda j: (0, j))],
        out_specs=pl.BlockSpec((rows, tn), lambda j: (0, j)),
        out_shape=jax.ShapeDtypeStruct((rows, n), F32),
        compiler_params=_params("arbitrary"),
        name="mod",
    )(cvec, w_mod, b_mod.reshape(1, n))


N_F32_TILES = 7
N_BF16_TILES = 4
TILE_N = 1024


def _inproj_kernel(x_ref, n1_ref, sh_ref, sc_ref, w_ref, b_ref, wg_ref, bg_ref, cos_ref, sin_ref,
                   of_ref, ob_ref, og_ref, h_ref):
    j = pl.program_id(1)

    @pl.when(j == 0)
    def _():
        x = x_ref[...]
        y = x * lax.rsqrt(jnp.mean(x * x, axis=-1, keepdims=True) + EPS) * n1_ref[...]
        h = (y * (1.0 + sc_ref[...]) + sh_ref[...]).astype(BF16)
        h_ref[...] = h
        og_ref[...] = jnp.dot(h, wg_ref[...], preferred_element_type=F32) + bg_ref[...]

    acc = jnp.dot(h_ref[...], w_ref[...], preferred_element_type=F32) + b_ref[...]

    @pl.when(j < N_F32_TILES)
    def _():
        of_ref[...] = acc

    @pl.when(jnp.logical_or(j == N_F32_TILES, j == N_F32_TILES + 3))
    def _():
        ob_ref[...] = acc.astype(BF16)

    def rope(scale):
        cos = cos_ref[...]
        sin = sin_ref[...]
        lane = lax.broadcasted_iota(jnp.int32, cos.shape, 1)
        first = (lane % (2 * (A_HALF_DIM // 2))) < (A_HALF_DIM // 2)
        for hh in range(TILE_N // LANES):
            a = acc[:, hh * LANES:(hh + 1) * LANES]
            r = jnp.where(first, pltpu.roll(a, LANES - A_HALF_DIM // 2, 1), pltpu.roll(a, A_HALF_DIM // 2, 1))
            ob_ref[:, hh * LANES:(hh + 1) * LANES] = ((a * cos + r * sin) * scale).astype(BF16)

    @pl.when(j == N_F32_TILES + 1)
    def _():
        rope(A_HALF_DIM ** -0.5)

    @pl.when(j == N_F32_TILES + 2)
    def _():
        rope(1.0)


def _inproj(x2, mod3, seg_of_block, norm_w, w_main, b_main, w_g, b_g, cos, sin, tm):
    rows, d = x2.shape
    nj = N_F32_TILES + N_BF16_TILES
    grid = (rows // tm, nj)
    return pl.pallas_call(
        _inproj_kernel,
        grid=grid,
        in_specs=[
            pl.BlockSpec((tm, d), lambda i, j: (i, 0)),
            pl.BlockSpec((1, d), lambda i, j: (0, 0)),
            pl.BlockSpec((None, 1, d), lambda i, j: (seg_of_block(i), 0, 0)),
            pl.BlockSpec((None, 1, d), lambda i, j: (seg_of_block(i), 0, 1)),
            pl.BlockSpec((d, TILE_N), lambda i, j: (0, j)),
            pl.BlockSpec((1, TILE_N), lambda i, j: (0, j)),
            pl.BlockSpec((d, 2 * LANES), lambda i, j: (0, 0)),
            pl.BlockSpec((1, 2 * LANES), lambda i, j: (0, 0)),
            pl.BlockSpec((tm, LANES), lambda i, j: (i, 0)),
            pl.BlockSpec((tm, LANES), lambda i, j: (i, 0)),
        ],
        out_specs=[
            pl.BlockSpec((tm, TILE_N), lambda i, j: (i, jnp.minimum(j, N_F32_TILES - 1))),
            pl.BlockSpec((tm, TILE_N), lambda i, j: (i, jnp.maximum(j - N_F32_TILES, 0))),
            pl.BlockSpec((tm, 2 * LANES), lambda i, j: (i, 0)),
        ],
        out_shape=[
            jax.ShapeDtypeStruct((rows, N_F32_TILES * TILE_N), F32),
            jax.ShapeDtypeStruct((rows, N_BF16_TILES * TILE_N), BF16),
            jax.ShapeDtypeStruct((rows, 2 * LANES), F32),
        ],
        scratch_shapes=[pltpu.VMEM((tm, d), BF16)],
        compiler_params=_params("arbitrary", "arbitrary"),
        name="inproj",
    )(x2, norm_w, mod3, mod3, w_main, b_main, w_g, b_g, cos, sin)


QM_TILE = 4


def _conv_kernel(prev_ref, cur_ref, next_ref, w_ref, b_ref, o_ref, buf_ref, *, q_scale):
    i = pl.program_id(1)
    n = pl.num_programs(1)
    jc = pl.program_id(2)
    tc = cur_ref.shape[0]
    halo = SUBLANES
    buf_ref[0:halo, :] = jnp.where(i > 0, prev_ref[...], 0.0)
    buf_ref[halo:halo + tc, :] = cur_ref[...]
    buf_ref[halo + tc:2 * halo + tc, :] = jnp.where(i < n - 1, next_ref[...], 0.0)
    acc = jnp.zeros(cur_ref.shape, F32) + b_ref[...]
    for k in range(CONV_K):
        acc = acc + buf_ref[pl.ds(halo + k - CONV_K // 2, tc), :] * w_ref[k:k + 1, :]
    y = acc * _sigmoid(acc)
    y = y * jnp.where(jc == 0, q_scale, 1.0)
    o_ref[...] = y.astype(BF16)


def _conv(of3, conv_w, conv_b, tc):
    b, t, ncol = of3.shape
    w = M_HEADS * M_HEAD_DIM
    of4 = of3.reshape(b, t // SUBLANES, SUBLANES, ncol)
    g = tc // SUBLANES
    last = t // SUBLANES - 1
    return pl.pallas_call(
        functools.partial(_conv_kernel, q_scale=M_HEAD_DIM ** -0.5),
        grid=(b, t // tc, 2),
        in_specs=[
            pl.BlockSpec((None, None, SUBLANES, w), lambda bb, i, jc: (bb, jnp.maximum(i * g - 1, 0), 0, QM_TILE + jc)),
            pl.BlockSpec((None, tc, w), lambda bb, i, jc: (bb, i, QM_TILE + jc)),
            pl.BlockSpec((None, None, SUBLANES, w), lambda bb, i, jc: (bb, jnp.minimum((i + 1) * g, last), 0, QM_TILE + jc)),
            pl.BlockSpec((CONV_K, w), lambda bb, i, jc: (0, jc)),
            pl.BlockSpec((1, w), lambda bb, i, jc: (0, jc)),
        ],
        out_specs=pl.BlockSpec((None, tc, w), lambda bb, i, jc: (bb, i, jc)),
        out_shape=jax.ShapeDtypeStruct((b, t, 2 * w), BF16),
        scratch_shapes=[pltpu.VMEM((tc + 2 * SUBLANES, w), F32)],
        compiler_params=_params("arbitrary", "arbitrary", "arbitrary"),
        name="conv",
    )(of4, of3, of4, conv_w, conv_b.reshape(1, 2 * w))


def _split3(x):
    hi = x.astype(BF16)
    r = x - hi.astype(F32)
    mid = r.astype(BF16)
    lo = (r - mid.astype(F32)).astype(BF16)
    return hi, mid, lo


def _log_sigmoid(x):
    return jnp.minimum(x, 0.0) - jnp.log1p(jnp.exp(-jnp.abs(x)))


def _mlstm_kernel(qc_ref, kc_ref, vc_ref, gc_ref, ql_ref, kl_ref, vl_ref, gl_ref, o_ref,
                  c_ref, n_ref, m_ref, *, n_ctx_chunks):
    d = pl.program_id(1)
    c = pl.program_id(2)
    L = M_CHUNK
    dh = M_HEAD_DIM

    @pl.when(c == 0)
    def _():
        c_ref[...] = jnp.zeros(c_ref.shape, F32)
        n_ref[...] = jnp.zeros(n_ref.shape, F32)
        m_ref[...] = jnp.zeros(m_ref.shape, F32)

    def step(q_ref, k_ref, v_ref, g_ref, with_output):
        row = lax.broadcasted_iota(jnp.int32, (L, L), 0)
        col = lax.broadcasted_iota(jnp.int32, (L, L), 1)
        causal = (row - col) * (1 - 2 * d) >= 0
        tri = causal.astype(BF16)
        g = g_ref[...]
        lf = _log_sigmoid(g)
        hi, mid, lo = _split3(lf)
        b = (jnp.dot(tri, hi, preferred_element_type=F32) + jnp.dot(tri, mid, preferred_element_type=F32)
             + jnp.dot(tri, lo, preferred_element_type=F32))
        btot = jnp.sum(lf, axis=0, keepdims=True)
        b_t = b.T
        g_t = g.T
        for h in range(M_HEADS):
            hs = slice(h * dh, (h + 1) * dh)
            q = q_ref[:, hs]
            k = k_ref[:, hs]
            v = v_ref[:, hs]
            i_c = g[:, h:h + 1]
            b_c = b[:, M_HEADS + h:M_HEADS + h + 1]
            i_r = g_t[h:h + 1, :]
            b_r = b_t[M_HEADS + h:M_HEADS + h + 1, :]
            b_end = btot[:, M_HEADS + h:M_HEADS + h + 1]
            m_prev = m_ref[h][:, 0:1]
            if with_output:
                log_d = jnp.where(causal, b_c - b_r + i_r, -jnp.inf)
                m_t = jnp.maximum(b_c + m_prev, jnp.max(log_d, axis=1, keepdims=True))
                inter = jnp.exp(b_c + m_prev - m_t)
                dmat = jnp.exp(log_d - m_t)
                s = lax.dot_general(q, k, NT_DIMS, preferred_element_type=F32)
                qk = s * dmat
                num = (inter * jnp.dot(q, c_ref[h].astype(BF16), preferred_element_type=F32)
                       + jnp.dot(qk.astype(BF16), v, preferred_element_type=F32))
                qn = jnp.sum(q.astype(F32) * n_ref[h], axis=1, keepdims=True)
                den = inter * qn + jnp.sum(qk, axis=1, keepdims=True)
                o_ref[:, hs] = num / jnp.maximum(jnp.abs(den), jnp.exp(-m_t))
            lw_c = b_end - b_c + i_c
            lw_r = b_end - b_r + i_r
            m_end = jnp.maximum(b_end + m_prev, jnp.max(lw_r, axis=1, keepdims=True))
            w_c = jnp.exp(lw_c - m_end)
            decay = jnp.exp(b_end + m_prev - m_end)
            kw = k.astype(F32) * w_c
            c_ref[h] = decay * c_ref[h] + lax.dot_general(kw.astype(BF16), v, TN_DIMS, preferred_element_type=F32)
            n_ref[h] = decay * n_ref[h] + jnp.sum(kw, axis=0, keepdims=True)
            m_ref[h] = jnp.broadcast_to(m_end, (1, LANES))

    @pl.when(c < n_ctx_chunks)
    def _():
        step(qc_ref, kc_ref, vc_ref, gc_ref, False)

    @pl.when(c >= n_ctx_chunks)
    def _():
        step(ql_ref, kl_ref, vl_ref, gl_ref, True)


def _mlstm(qk_ctx, ob_ctx, og_ctx, qk_lat, ob_lat, og_lat):
    b, tc_, _ = qk_ctx.shape
    t = qk_lat.shape[1]
    w = M_HEADS * M_HEAD_DIM
    ncc = tc_ // M_CHUNK
    ncl = t // M_CHUNK

    def ctx_idx(d, c):
        cc = jnp.minimum(c, ncc - 1)
        return jnp.where(d == 0, cc, ncc - 1 - cc)

    def lat_idx(d, c):
        cl = jnp.maximum(c - ncc, 0)
        return jnp.where(d == 0, cl, ncl - 1 - cl)

    def cspec(width, colfn):
        return pl.BlockSpec((None, M_CHUNK, width), lambda bb, d, c: (bb, ctx_idx(d, c), colfn(d)))

    def lspec(width, colfn):
        return pl.BlockSpec((None, M_CHUNK, width), lambda bb, d, c: (bb, lat_idx(d, c), colfn(d)))

    return pl.pallas_call(
        functools.partial(_mlstm_kernel, n_ctx_chunks=ncc),
        grid=(b, 2, ncc + ncl),
        in_specs=[cspec(w, lambda d: 0), cspec(w, lambda d: 1), cspec(w, lambda d: 0), cspec(LANES, lambda d: d),
                  lspec(w, lambda d: 0), lspec(w, lambda d: 1), lspec(w, lambda d: 0), lspec(LANES, lambda d: d)],
        out_specs=pl.BlockSpec((None, None, M_CHUNK, w), lambda bb, d, c: (d, bb, lat_idx(d, c), 0)),
        out_shape=jax.ShapeDtypeStruct((2, b, t, w), F32),
        scratch_shapes=[pltpu.VMEM((M_HEADS, M_HEAD_DIM, M_HEAD_DIM), F32),
                        pltpu.VMEM((M_HEADS, 1, M_HEAD_DIM), F32),
                        pltpu.VMEM((M_HEADS, 1, LANES), F32)],
        compiler_params=_params("arbitrary", "arbitrary", "arbitrary"),
        name="mlstm",
    )(qk_ctx, qk_ctx, ob_ctx, og_ctx, qk_lat, qk_lat, ob_lat, og_lat)


Q_TILE_COL = 8
K_TILE_COL = 16
V_TILE_COL = 24


def _attn_kernel(lam_ref, q_ref, k_ref, v_ref, kc_ref, vc_ref, o_ref, qs_ref, m_ref, l_ref, acc_ref, *, tk):
    tq = q_ref.shape[0]
    q = q_ref[...]
    lane = lax.broadcasted_iota(jnp.int32, q.shape, 1)
    zero = jnp.zeros_like(q)
    qs_ref[0:tq, :] = jnp.where(lane < A_HALF_DIM, q, zero)
    qs_ref[tq:2 * tq, :] = jnp.where(lane >= A_HALF_DIM, q, zero)

    def scores(kb):
        return lax.dot_general(qs_ref[...], kb, NT_DIMS, preferred_element_type=F32)

    s = scores(kc_ref[...])
    m0 = jnp.max(s, axis=1, keepdims=True)
    p = jnp.exp(s - m0)
    m_ref[...] = m0
    l_ref[...] = jnp.sum(p, axis=1, keepdims=True)
    acc_ref[...] = jnp.dot(p.astype(BF16), vc_ref[...], preferred_element_type=F32)

    def body(i, carry):
        off = pl.multiple_of(i * tk, tk)
        s = scores(k_ref[pl.ds(off, tk), :])
        m_prev = m_ref[...]
        m_new = jnp.maximum(m_prev, jnp.max(s, axis=1, keepdims=True))
        alpha = jnp.exp(m_prev - m_new)
        p = jnp.exp(s - m_new)
        l_ref[...] = alpha * l_ref[...] + jnp.sum(p, axis=1, keepdims=True)
        acc_ref[...] = alpha * acc_ref[...] + jnp.dot(p.astype(BF16), v_ref[pl.ds(off, tk), :],
                                                      preferred_element_type=F32)
        m_ref[...] = m_new
        return carry

    lax.fori_loop(0, k_ref.shape[0] // tk, body, 0)
    o = acc_ref[...] / l_ref[...]
    o_ref[...] = o[0:tq] - lam_ref[0, 0] * o[tq:2 * tq]


def _attn(lam, ob_lat, ob_ctx, tq, tk):
    b, t, _ = ob_lat.shape
    tc_ = ob_ctx.shape[1]
    w = A_HEADS * 2 * A_HALF_DIM
    return pl.pallas_call(
        functools.partial(_attn_kernel, tk=tk),
        grid=(b, A_HEADS, t // tq),
        in_specs=[
            pl.BlockSpec(memory_space=pltpu.SMEM),
            pl.BlockSpec((None, tq, LANES), lambda bb, h, i: (bb, i, Q_TILE_COL + h)),
            pl.BlockSpec((None, t, LANES), lambda bb, h, i: (bb, 0, K_TILE_COL + h)),
            pl.BlockSpec((None, t, LANES), lambda bb, h, i: (bb, 0, V_TILE_COL + h)),
            pl.BlockSpec((None, tc_, LANES), lambda bb, h, i: (bb, 0, K_TILE_COL + h)),
            pl.BlockSpec((None, tc_, LANES), lambda bb, h, i: (bb, 0, V_TILE_COL + h)),
        ],
        out_specs=pl.BlockSpec((None, tq, LANES), lambda bb, h, i: (bb, i, h)),
        out_shape=jax.ShapeDtypeStruct((b, t, w), F32),
        scratch_shapes=[pltpu.VMEM((2 * tq, LANES), BF16),
                        pltpu.VMEM((2 * tq, 1), F32),
                        pltpu.VMEM((2 * tq, 1), F32),
                        pltpu.VMEM((2 * tq, LANES), F32)],
        compiler_params=_params("arbitrary", "arbitrary", "arbitrary"),
        name="attn",
    )(lam, ob_lat, ob_lat, ob_lat, ob_ctx, ob_ctx)


GA_TILE = 0
GB_TILE = 1
ZM_TILE = 6


def _head_rmsnorm(x, w, n_heads, dh):
    outs = []
    for h in range(n_heads):
        blk = x[:, h * dh:(h + 1) * dh]
        outs.append(blk * lax.rsqrt(jnp.mean(blk * blk, axis=-1, keepdims=True) + EPS))
    return jnp.concatenate(outs, axis=1) * w


def _merge_kernel(hf_ref, hb_ref, z_ref, o_ref, ga_ref, gb_ref, mnw_ref, anw_ref, wpa_ref, wpb_ref, out_ref,
                  *, d_scale):
    hsum = hf_ref[...] + hb_ref[...]
    ym = _sigmoid(z_ref[...]) * _head_rmsnorm(hsum, mnw_ref[...], M_HEADS, M_HEAD_DIM)
    yd = _head_rmsnorm(o_ref[...], anw_ref[...], A_HEADS, 2 * A_HALF_DIM) * d_scale
    pa = jnp.dot(ym.astype(BF16), wpa_ref[...], preferred_element_type=F32)
    pb = jnp.dot(yd.astype(BF16), wpb_ref[...], preferred_element_type=F32)
    out_ref[...] = (_sigmoid(ga_ref[...]) * pa + _sigmoid(gb_ref[...]) * pb).astype(BF16)


def _merge(hdir, of_lat, o_attn, mnw, anw, w_pa, w_pb, d_scale, tm):
    rows = of_lat.shape[0]
    w = M_HEADS * M_HEAD_DIM
    d = w_pa.shape[1]
    return pl.pallas_call(
        functools.partial(_merge_kernel, d_scale=d_scale),
        grid=(rows // tm,),
        in_specs=[
            pl.BlockSpec((None, tm, w), lambda i: (0, i, 0)),
            pl.BlockSpec((None, tm, w), lambda i: (1, i, 0)),
            pl.BlockSpec((tm, w), lambda i: (i, ZM_TILE)),
            pl.BlockSpec((tm, w), lambda i: (i, 0)),
            pl.BlockSpec((tm, d), lambda i: (i, GA_TILE)),
            pl.BlockSpec((tm, d), lambda i: (i, GB_TILE)),
            pl.BlockSpec((1, w), lambda i: (0, 0)),
            pl.BlockSpec((1, w), lambda i: (0, 0)),
            pl.BlockSpec((w, d), lambda i: (0, 0)),
            pl.BlockSpec((w, d), lambda i: (0, 0)),
        ],
        out_specs=pl.BlockSpec((tm, d), lambda i: (i, 0)),
        out_shape=jax.ShapeDtypeStruct((rows, d), BF16),
        compiler_params=_params("arbitrary"),
        name="merge",
    )(hdir, hdir, of_lat, o_attn, of_lat, of_lat, mnw, anw, w_pa, w_pb)


def _resid_kernel(mg_ref, wout_ref, x_ref, g1_ref, n2_ref, sh_ref, sc_ref, x1_ref, h2_ref):
    y = jnp.dot(mg_ref[...], wout_ref[...], preferred_element_type=F32)
    x1 = x_ref[...] + g1_ref[...] * y
    x1_ref[...] = x1
    hn = x1 * lax.rsqrt(jnp.mean(x1 * x1, axis=-1, keepdims=True) + EPS) * n2_ref[...]
    h2_ref[...] = (hn * (1.0 + sc_ref[...]) + sh_ref[...]).astype(BF16)


def _resid(merged, w_out, x2, mod3, norm2_w, blocks_per_batch, tm):
    rows, d = x2.shape

    def mspec(col):
        return pl.BlockSpec((None, 1, d), lambda i: (i // blocks_per_batch, 0, col))

    return pl.pallas_call(
        _resid_kernel,
        grid=(rows // tm,),
        in_specs=[
            pl.BlockSpec((tm, d), lambda i: (i, 0)),
            pl.BlockSpec((d, d), lambda i: (0, 0)),
            pl.BlockSpec((tm, d), lambda i: (i, 0)),
            mspec(2),
            pl.BlockSpec((1, d), lambda i: (0, 0)),
            mspec(3),
            mspec(4),
        ],
        out_specs=[pl.BlockSpec((tm, d), lambda i: (i, 0)), pl.BlockSpec((tm, d), lambda i: (i, 0))],
        out_shape=[jax.ShapeDtypeStruct((rows, d), F32), jax.ShapeDtypeStruct((rows, d), BF16)],
        compiler_params=_params("arbitrary"),
        name="resid",
    )(merged, w_out, x2, mod3, norm2_w, mod3, mod3)


N_TOP = P_TOPK + 1
TOP_ROWS = 24
CAND_COUNTS = tuple(N_TOP // r for r in range(1, N_TOP + 1))
N_CAND = sum(CAND_COUNTS)
CAND_ROWS = -(-N_CAND // SUBLANES) * SUBLANES


def _extract_top(s, count, store):
    rows = s.shape[0]
    ridx = lax.broadcasted_iota(jnp.int32, s.shape, 0)
    for r in range(count):
        m = jnp.max(s, axis=0, keepdims=True)
        first = jnp.min(jnp.where(s == m, ridx, rows), axis=0, keepdims=True)
        s = jnp.where(ridx == first, -jnp.inf, s)
        store(r, m)


def _psel_kernel(h_ref, wpq_ref, keys_ref, s2_ref, e2_ref, thr_ref, e1_ref, q_scr, top_scr, cand_scr, ctop_scr):
    nk = keys_ref.shape[1]
    q = jnp.dot(h_ref[...], wpq_ref[...], preferred_element_type=F32).astype(BF16)
    for hp in range(2 * P_HEADS):
        q_scr[hp] = q[:, hp * LANES:(hp + 1) * LANES]
    cand_scr[...] = jnp.full(cand_scr.shape, -jnp.inf, F32)
    top_scr[...] = jnp.zeros(top_scr.shape, F32)
    ctop_scr[...] = jnp.zeros(ctop_scr.shape, F32)

    def head(hh, carry):
        s_half = []
        for p in range(2):
            s = lax.dot_general(keys_ref[2 * hh + p], q_scr[2 * hh + p], NT_DIMS, preferred_element_type=F32)
            s_half.append(s)

            def store_top(r, row, p=p):
                top_scr[p, pl.ds(r, 1), :] = row

            _extract_top(s, N_TOP, store_top)
        off = 0
        for r1, cnt in enumerate(CAND_COUNTS):
            cand_scr[pl.ds(off, cnt), :] = top_scr[0, pl.ds(r1, 1), :] + top_scr[1, pl.ds(0, cnt), :]
            off += cnt

        def store_ctop(r, row):
            ctop_scr[pl.ds(r, 1), :] = row

        _extract_top(cand_scr[...], N_TOP, store_ctop)
        ctop = ctop_scr[...]
        c_max = ctop[0:1, :]
        z = jnp.sum(jnp.exp(ctop[0:P_TOPK, :] - c_max), axis=0, keepdims=True)
        tau = 0.5 * (ctop[P_TOPK - 1:P_TOPK, :] + ctop[P_TOPK:P_TOPK + 1, :])
        s1, s2 = s_half
        s2_ref[hh] = s2
        e2_ref[hh] = jnp.exp(s2 - top_scr[1, pl.ds(0, 1), :])
        thr_ref[hh] = tau - s1
        e1_ref[hh] = jnp.exp(s1 - top_scr[0, pl.ds(0, 1), :]) / z
        return carry

    lax.fori_loop(0, P_HEADS, head, 0)


def _psel(h2, w_pq, keys, tm):
    rows, d = h2.shape
    nk = keys.shape[1]
    nq = w_pq.shape[1]
    shp = jax.ShapeDtypeStruct((P_HEADS, nk, rows), F32)
    ospec = pl.BlockSpec((P_HEADS, nk, tm), lambda i: (0, 0, i))
    return pl.pallas_call(
        _psel_kernel,
        grid=(rows // tm,),
        in_specs=[
            pl.BlockSpec((tm, d), lambda i: (i, 0)),
            pl.BlockSpec((d, nq), lambda i: (0, 0)),
            pl.BlockSpec((2 * P_HEADS, nk, LANES), lambda i: (0, 0, 0)),
        ],
        out_specs=[ospec, ospec, ospec, ospec],
        out_shape=[shp, shp, shp, shp],
        scratch_shapes=[pltpu.VMEM((2 * P_HEADS, tm, LANES), BF16),
                        pltpu.VMEM((2, TOP_ROWS, tm), F32),
                        pltpu.VMEM((CAND_ROWS, tm), F32),
                        pltpu.VMEM((TOP_ROWS, tm), F32)],
        compiler_params=_params("arbitrary"),
        name="psel",
    )(h2, w_pq, keys)


def _peer_kernel(h_ref, u_ref, vt_ref, s2_ref, e2_ref, thr_ref, e1_ref, o_ref, w_scr, *, te):
    k = pl.program_id(1)
    nk = s2_ref.shape[1]

    @pl.when(k == 0)
    def _():
        o_ref[...] = jnp.zeros(o_ref.shape, F32)

    pre = lax.dot_general(u_ref[...], h_ref[...], NT_DIMS, preferred_element_type=F32)
    act = 0.5 * pre * (1.0 + lax.erf(pre * np.float32(math.sqrt(0.5))))
    for ab in range(te // nk):
        a = k * (te // nk) + ab
        g = jnp.zeros((nk, h_ref.shape[0]), F32)
        for hh in range(P_HEADS):
            thr = thr_ref[hh, pl.ds(a, 1), :]
            e1 = e1_ref[hh, pl.ds(a, 1), :]
            g = g + jnp.where(s2_ref[hh] >= thr, e2_ref[hh], 0.0) * e1
        w_scr[ab * nk:(ab + 1) * nk, :] = (g * act[ab * nk:(ab + 1) * nk, :]).astype(BF16)
    o_ref[...] += jnp.dot(vt_ref[...], w_scr[...], preferred_element_type=F32)


def _peer(h2, u_bf, vt_bf, s2, e2, thr, e1, tm, te):
    rows, d = h2.shape
    ne = u_bf.shape[0]
    nk = s2.shape[1]
    sspec = pl.BlockSpec((P_HEADS, nk, tm), lambda i, k: (0, 0, i))
    return pl.pallas_call(
        functools.partial(_peer_kernel, te=te),
        grid=(rows // tm, ne // te),
        in_specs=[
            pl.BlockSpec((tm, d), lambda i, k: (i, 0)),
            pl.BlockSpec((te, d), lambda i, k: (k, 0)),
            pl.BlockSpec((d, te), lambda i, k: (0, k)),
            sspec, sspec, sspec, sspec,
        ],
        out_specs=pl.BlockSpec((d, tm), lambda i, k: (0, i)),
        out_shape=jax.ShapeDtypeStruct((d, rows), F32),
        scratch_shapes=[pltpu.VMEM((te, tm), BF16)],
        compiler_params=_params("arbitrary", "arbitrary"),
        name="peer",
    )(h2, u_bf, vt_bf, s2, e2, thr, e1)


def _final_kernel(x1_ref, pt_ref, g2_ref, fw_ref, o_ref):
    x2 = x1_ref[...] + g2_ref[...] * pt_ref[...].T
    o_ref[...] = x2 * lax.rsqrt(jnp.mean(x2 * x2, axis=-1, keepdims=True) + EPS) * fw_ref[...]


def _final(x1, peer_t, mod3, final_w, blocks_per_batch, tm):
    rows, d = x1.shape
    return pl.pallas_call(
        _final_kernel,
        grid=(rows // tm,),
        in_specs=[
            pl.BlockSpec((tm, d), lambda i: (i, 0)),
            pl.BlockSpec((d, tm), lambda i: (0, i)),
            pl.BlockSpec((None, 1, d), lambda i: (i // blocks_per_batch, 0, 5)),
            pl.BlockSpec((1, d), lambda i: (0, 0)),
        ],
        out_specs=pl.BlockSpec((tm, d), lambda i: (i, 0)),
        out_shape=jax.ShapeDtypeStruct((rows, d), F32),
        compiler_params=_params("arbitrary"),
        name="final",
    )(x1, peer_t, mod3, final_w)


def _rope_tables(t):
    rows = t // GRID_W
    row = jnp.repeat(jnp.arange(rows, dtype=F32), GRID_W)
    col = jnp.tile(jnp.arange(GRID_W, dtype=F32), rows)
    axis_dim = A_HALF_DIM // 2
    inv_freq = ROPE_BASE ** (-jnp.arange(0, axis_dim, 2, dtype=F32) / axis_dim)
    ang = jnp.concatenate([row[:, None] * inv_freq, col[:, None] * inv_freq], axis=-1)
    cos = jnp.tile(jnp.cos(ang), (1, LANES // axis_dim))
    sin = jnp.tile(jnp.concatenate([-jnp.sin(ang), jnp.sin(ang)], axis=-1), (1, LANES // (2 * axis_dim)))
    return cos, sin


def kernel(x, c, ctx, c_ctx, w_mod, b_mod, norm1_w, w_in, b_in, conv_w, conv_b, m_norm_w, lambdas, a_norm_w,
           w_pa, w_pb, w_out, norm2_w, w_pq, sub_keys, expert_u, expert_v, final_norm_w):
    depth = w_mod.shape[0]
    assert depth == 1, "single-layer configuration only (context outputs never reach a latent token)"
    bsz, t, d = x.shape
    t_ctx = ctx.shape[1]
    mw = M_HEADS * M_HEAD_DIM
    aw = A_HEADS * 2 * A_HALF_DIM
    assert w_in.shape[2] == 4 * mw + 4 * M_HEADS + 3 * aw + 2 * d and d == 2 * TILE_N and mw == TILE_N == aw
    lam_init = 0.8 - 0.6 * math.exp(-0.3 * 0)

    mod_rows = -(-(bsz + 1) // SUBLANES) * SUBLANES
    cvec = jnp.zeros((mod_rows, d), F32).at[:bsz].set(c).at[bsz].set(c_ctx)
    mod3 = _mod(cvec, w_mod[0], b_mod[0]).reshape(mod_rows, 1, 6 * d)

    wi, bi = w_in[0], b_in[0]
    offs = np.cumsum((0, mw, mw, mw, mw, 4 * M_HEADS, aw, aw, aw, d, d))
    seg = {n: slice(int(offs[i]), int(offs[i + 1]))
           for i, n in enumerate(("qm", "km", "vm", "zm", "g", "qa", "ka", "va", "ga", "gb"))}
    order = ("ga", "gb", "qm", "km", "zm", "vm", "qa", "ka", "va")
    w_main = jnp.concatenate([wi[:, seg[n]] for n in order], axis=1).astype(BF16)
    b_main = jnp.concatenate([bi[seg[n]] for n in order]).reshape(1, -1)
    gpad = LANES - 2 * M_HEADS
    wg, bg = wi[:, seg["g"]], bi[seg["g"]]
    w_g = jnp.concatenate([wg[:, :2 * M_HEADS], jnp.zeros((d, gpad), F32),
                           wg[:, 2 * M_HEADS:], jnp.zeros((d, gpad), F32)], axis=1).astype(BF16)
    b_g = jnp.concatenate([bg[:2 * M_HEADS], jnp.zeros((gpad,), F32),
                           bg[2 * M_HEADS:], jnp.zeros((gpad,), F32)]).reshape(1, -1)

    cos, sin = _rope_tables(t)
    cos_l = jnp.tile(cos, (bsz, 1))
    sin_l = jnp.tile(sin, (bsz, 1))
    n1 = norm1_w[0].reshape(1, d)

    tm_in = 512
    blocks_per_batch = t // tm_in
    of_lat, ob_lat, og_lat = _inproj(x.reshape(bsz * t, d), mod3, lambda i: i // blocks_per_batch, n1,
                                     w_main, b_main, w_g, b_g, cos_l, sin_l, tm_in)
    rows_c = bsz * t_ctx
    of_ctx, ob_ctx, og_ctx = _inproj(ctx.reshape(rows_c, d), mod3, lambda i: bsz, n1, w_main, b_main, w_g, b_g,
                                     jnp.ones((rows_c, LANES), F32), jnp.zeros((rows_c, LANES), F32), rows_c)

    qk_lat = _conv(of_lat.reshape(bsz, t, -1), conv_w[0], conv_b[0], 512)
    qk_ctx = _conv(of_ctx.reshape(bsz, t_ctx, -1), conv_w[0], conv_b[0], t_ctx)
    ob_lat3 = ob_lat.reshape(bsz, t, -1)
    ob_ctx3 = ob_ctx.reshape(bsz, t_ctx, -1)
    hdir = _mlstm(qk_ctx, ob_ctx3, og_ctx.reshape(bsz, t_ctx, -1), qk_lat, ob_lat3, og_lat.reshape(bsz, t, -1))

    lq1, lk1, lq2, lk2 = lambdas[0].astype(F32)
    lam = (jnp.exp(jnp.sum(lq1 * lk1)) - jnp.exp(jnp.sum(lq2 * lk2)) + lam_init).reshape(1, 1)
    o_attn = _attn(lam, ob_lat3, ob_ctx3, 256, 512)

    merged = _merge(hdir.reshape(2, bsz * t, mw), of_lat, o_attn.reshape(bsz * t, aw),
                    m_norm_w[0].reshape(1, mw), jnp.tile(a_norm_w[0], A_HEADS).reshape(1, aw),
                    w_pa[0].astype(BF16), w_pb[0].astype(BF16), 1.0 - lam_init, 256)
    tm_r = 256
    x1, h2 = _resid(merged, w_out[0].astype(BF16), x.reshape(bsz * t, d), mod3, norm2_w[0].reshape(1, d),
                    t // tm_r, tm_r)

    nk = sub_keys.shape[3]
    keys = sub_keys[0].reshape(2 * P_HEADS, nk, sub_keys.shape[4]).astype(BF16)
    s2, e2, thr, e1 = _psel(h2, w_pq[0].astype(BF16), keys, 256)
    tm_p = 512
    peer_t = _peer(h2, expert_u[0].astype(BF16), expert_v[0].T.astype(BF16), s2, e2, thr, e1, tm_p, 512)

    tm_f = 256
    out = _final(x1, peer_t, mod3, final_norm_w.reshape(1, d), t // tm_f, tm_f)
    return out.reshape(bsz, t, d)
```

```python
import functools
import math

import jax
import jax.numpy as jnp
import numpy as np
from jax import lax
from jax.experimental import pallas as pl
from jax.experimental.pallas import tpu as pltpu

F32 = jnp.float32
BF16 = jnp.bfloat16

EPS = 1e-6
GRID_W = 64
ROPE_BASE = 10000.0
M_HEADS = 8
M_HEAD_DIM = 128
M_CHUNK = 128
CONV_K = 5
A_HEADS = 8
A_HALF_DIM = 64
P_HEADS = 8
P_TOPK = 16
LANES = 128
SUBLANES = 8
VMEM_LIMIT = 56 * 1024 * 1024

NT_DIMS = (((1,), (1,)), ((), ()))
TN_DIMS = (((0,), (0,)), ((), ()))


def _params(*sem):
    return pltpu.CompilerParams(dimension_semantics=sem, vmem_limit_bytes=VMEM_LIMIT)


def _sigmoid(x):
    return 1.0 / (1.0 + jnp.exp(-x))


def _mod_kernel(c_ref, w_ref, b_ref, o_ref):
    c = c_ref[...]
    s = (c * _sigmoid(c)).astype(BF16)
    o_ref[...] = jnp.dot(s, w_ref[...].astype(BF16), preferred_element_type=F32) + b_ref[...]


def _mod(cvec, w_mod, b_mod):
    rows, d = cvec.shape
    n = w_mod.shape[1]
    tn = 1024
    return pl.pallas_call(
        _mod_kernel,
        grid=(n // tn,),
        in_specs=[pl.BlockSpec((rows, d), lambda j: (0, 0)),
                  pl.BlockSpec((d, tn), lambda j: (0, j)),
                  pl.BlockSpec((1, tn), lambda j: (0, j))],
        out_specs=pl.BlockSpec((rows, tn), lambda j: (0, j)),
        out_shape=jax.ShapeDtypeStruct((rows, n), F32),
        compiler_params=_params("arbitrary"),
        name="mod",
    )(cvec, w_mod, b_mod.reshape(1, n))


N_F32_TILES = 7
N_BF16_TILES = 4
TILE_N = 1024


def _inproj_kernel(x_ref, n1_ref, sh_ref, sc_ref, w_ref, b_ref, wg_ref, bg_ref, cos_ref, sin_ref,
                   of_ref, ob_ref, og_ref, h_ref):
    j = pl.program_id(1)

    @pl.when(j == 0)
    def _():
        x = x_ref[...]
        y = x * lax.rsqrt(jnp.mean(x * x, axis=-1, keepdims=True) + EPS) * n1_ref[...]
        h = (y * (1.0 + sc_ref[...]) + sh_ref[...]).astype(BF16)
        h_ref[...] = h
        og_ref[...] = jnp.dot(h, wg_ref[...], preferred_element_type=F32) + bg_ref[...]

    acc = jnp.dot(h_ref[...], w_ref[...], preferred_element_type=F32) + b_ref[...]

    @pl.when(j < N_F32_TILES)
    def _():
        of_ref[...] = acc

    @pl.when(jnp.logical_or(j == N_F32_TILES, j == N_F32_TILES + 3))
    def _():
        ob_ref[...] = acc.astype(BF16)

    def rope(scale):
        cos = cos_ref[...]
        sin = sin_ref[...]
        lane = lax.broadcasted_iota(jnp.int32, cos.shape, 1)
        first = (lane % (2 * (A_HALF_DIM // 2))) < (A_HALF_DIM // 2)
        for hh in range(TILE_N // LANES):
            a = acc[:, hh * LANES:(hh + 1) * LANES]
            r = jnp.where(first, pltpu.roll(a, LANES - A_HALF_DIM // 2, 1), pltpu.roll(a, A_HALF_DIM // 2, 1))
            ob_ref[:, hh * LANES:(hh + 1) * LANES] = ((a * cos + r * sin) * scale).astype(BF16)

    @pl.when(j == N_F32_TILES + 1)
    def _():
        rope(A_HALF_DIM ** -0.5)

    @pl.when(j == N_F32_TILES + 2)
    def _():
        rope(1.0)


def _inproj(x2, mod3, seg_of_block, norm_w, w_main, b_main, w_g, b_g, cos, sin, tm):
    rows, d = x2.shape
    nj = N_F32_TILES + N_BF16_TILES
    grid = (rows // tm, nj)
    return pl.pallas_call(
        _inproj_kernel,
        grid=grid,
        in_specs=[
            pl.BlockSpec((tm, d), lambda i, j: (i, 0)),
            pl.BlockSpec((1, d), lambda i, j: (0, 0)),
            pl.BlockSpec((None, 1, d), lambda i, j: (seg_of_block(i), 0, 0)),
            pl.BlockSpec((None, 1, d), lambda i, j: (seg_of_block(i), 0, 1)),
            pl.BlockSpec((d, TILE_N), lambda i, j: (0, j)),
            pl.BlockSpec((1, TILE_N), lambda i, j: (0, j)),
            pl.BlockSpec((d, 2 * LANES), lambda i, j: (0, 0)),
            pl.BlockSpec((1, 2 * LANES), lambda i, j: (0, 0)),
            pl.BlockSpec((tm, LANES), lambda i, j: (i, 0)),
            pl.BlockSpec((tm, LANES), lambda i, j: (i, 0)),
        ],
        out_specs=[
            pl.BlockSpec((tm, TILE_N), lambda i, j: (i, jnp.minimum(j, N_F32_TILES - 1))),
            pl.BlockSpec((tm, TILE_N), lambda i, j: (i, jnp.maximum(j - N_F32_TILES, 0))),
            pl.BlockSpec((tm, 2 * LANES), lambda i, j: (i, 0)),
        ],
        out_shape=[
            jax.ShapeDtypeStruct((rows, N_F32_TILES * TILE_N), F32),
            jax.ShapeDtypeStruct((rows, N_BF16_TILES * TILE_N), BF16),
            jax.ShapeDtypeStruct((rows, 2 * LANES), F32),
        ],
        scratch_shapes=[pltpu.VMEM((tm, d), BF16)],
        compiler_params=_params("arbitrary", "arbitrary"),
        name="inproj",
    )(x2, norm_w, mod3, mod3, w_main, b_main, w_g, b_g, cos, sin)


QM_TILE = 4


def _conv_kernel(prev_ref, cur_ref, next_ref, w_ref, b_ref, o_ref, buf_ref, *, q_scale):
    i = pl.program_id(1)
    n = pl.num_programs(1)
    jc = pl.program_id(2)
    tc = cur_ref.shape[0]
    halo = SUBLANES
    buf_ref[0:halo, :] = jnp.where(i > 0, prev_ref[...], 0.0)
    buf_ref[halo:halo + tc, :] = cur_ref[...]
    buf_ref[halo + tc:2 * halo + tc, :] = jnp.where(i < n - 1, next_ref[...], 0.0)
    acc = jnp.zeros(cur_ref.shape, F32) + b_ref[...]
    for k in range(CONV_K):
        acc = acc + buf_ref[pl.ds(halo + k - CONV_K // 2, tc), :] * w_ref[k:k + 1, :]
    y = acc * _sigmoid(acc)
    y = y * jnp.where(jc == 0, q_scale, 1.0)
    o_ref[...] = y.astype(BF16)


def _conv(of3, conv_w, conv_b, tc):
    b, t, ncol = of3.shape
    w = M_HEADS * M_HEAD_DIM
    of4 = of3.reshape(b, t // SUBLANES, SUBLANES, ncol)
    g = tc // SUBLANES
    last = t // SUBLANES - 1
    return pl.pallas_call(
        functools.partial(_conv_kernel, q_scale=M_HEAD_DIM ** -0.5),
        grid=(b, t // tc, 2),
        in_specs=[
            pl.BlockSpec((None, None, SUBLANES, w), lambda bb, i, jc: (bb, jnp.maximum(i * g - 1, 0), 0, QM_TILE + jc)),
            pl.BlockSpec((None, tc, w), lambda bb, i, jc: (bb, i, QM_TILE + jc)),
            pl.BlockSpec((None, None, SUBLANES, w), lambda bb, i, jc: (bb, jnp.minimum((i + 1) * g, last), 0, QM_TILE + jc)),
            pl.BlockSpec((CONV_K, w), lambda bb, i, jc: (0, jc)),
            pl.BlockSpec((1, w), lambda bb, i, jc: (0, jc)),
        ],
        out_specs=pl.BlockSpec((None, tc, w), lambda bb, i, jc: (bb, i, jc)),
        out_shape=jax.ShapeDtypeStruct((b, t, 2 * w), BF16),
        scratch_shapes=[pltpu.VMEM((tc + 2 * SUBLANES, w), F32)],
        compiler_params=_params("arbitrary", "arbitrary", "arbitrary"),
        name="conv",
    )(of4, of3, of4, conv_w, conv_b.reshape(1, 2 * w))


def _split3(x):
    hi = x.astype(BF16)
    r = x - hi.astype(F32)
    mid = r.astype(BF16)
    lo = (r - mid.astype(F32)).astype(BF16)
    return hi, mid, lo


def _log_sigmoid(x):
    return jnp.minimum(x, 0.0) - jnp.log1p(jnp.exp(-jnp.abs(x)))


def _mlstm_kernel(qc_ref, kc_ref, vc_ref, gc_ref, ql_ref, kl_ref, vl_ref, gl_ref, o_ref,
                  c_ref, n_ref, m_ref, *, n_ctx_chunks):
    d = pl.program_id(1)
    c = pl.program_id(2)
    L = M_CHUNK
    dh = M_HEAD_DIM

    @pl.when(c == 0)
    def _():
        c_ref[...] = jnp.zeros(c_ref.shape, F32)
        n_ref[...] = jnp.zeros(n_ref.shape, F32)
        m_ref[...] = jnp.zeros(m_ref.shape, F32)

    def step(q_ref, k_ref, v_ref, g_ref, with_output):
        row = lax.broadcasted_iota(jnp.int32, (L, L), 0)
        col = lax.broadcasted_iota(jnp.int32, (L, L), 1)
        causal = (row - col) * (1 - 2 * d) >= 0
        tri = causal.astype(BF16)
        g = g_ref[...]
        lf = _log_sigmoid(g)
        hi, mid, lo = _split3(lf)
        b = (jnp.dot(tri, hi, preferred_element_type=F32) + jnp.dot(tri, mid, preferred_element_type=F32)
             + jnp.dot(tri, lo, preferred_element_type=F32))
        btot = jnp.sum(lf, axis=0, keepdims=True)
        b_t = b.T
        g_t = g.T
        for h in range(M_HEADS):
            hs = slice(h * dh, (h + 1) * dh)
            q = q_ref[:, hs]
            k = k_ref[:, hs]
            v = v_ref[:, hs]
            i_c = g[:, h:h + 1]
            b_c = b[:, M_HEADS + h:M_HEADS + h + 1]
            i_r = g_t[h:h + 1, :]
            b_r = b_t[M_HEADS + h:M_HEADS + h + 1, :]
            b_end = btot[:, M_HEADS + h:M_HEADS + h + 1]
            m_prev = m_ref[h][:, 0:1]
            if with_output:
                log_d = jnp.where(causal, b_c - b_r + i_r, -jnp.inf)
                m_t = jnp.maximum(b_c + m_prev, jnp.max(log_d, axis=1, keepdims=True))
                inter = jnp.exp(b_c + m_prev - m_t)
                dmat = jnp.exp(log_d - m_t)
                s = lax.dot_general(q, k, NT_DIMS, preferred_element_type=F32)
                qk = s * dmat
                num = (inter * jnp.dot(q, c_ref[h].astype(BF16), preferred_element_type=F32)
                       + jnp.dot(qk.astype(BF16), v, preferred_element_type=F32))
                qn = jnp.sum(q.astype(F32) * n_ref[h], axis=1, keepdims=True)
                den = inter * qn + jnp.sum(qk, axis=1, keepdims=True)
                o_ref[:, hs] = num / jnp.maximum(jnp.abs(den), jnp.exp(-m_t))
            lw_c = b_end - b_c + i_c
            lw_r = b_end - b_r + i_r
            m_end = jnp.maximum(b_end + m_prev, jnp.max(lw_r, axis=1, keepdims=True))
            w_c = jnp.exp(lw_c - m_end)
            decay = jnp.exp(b_end + m_prev - m_end)
            kw = k.astype(F32) * w_c
            c_ref[h] = decay * c_ref[h] + lax.dot_general(kw.astype(BF16), v, TN_DIMS, preferred_element_type=F32)
            n_ref[h] = decay * n_ref[h] + jnp.sum(kw, axis=0, keepdims=True)
            m_ref[h] = jnp.broadcast_to(m_end, (1, LANES))

    @pl.when(c < n_ctx_chunks)
    def _():
        step(qc_ref, kc_ref, vc_ref, gc_ref, False)

    @pl.when(c >= n_ctx_chunks)
    def _():
        step(ql_ref, kl_ref, vl_ref, gl_ref, True)


def _mlstm(qk_ctx, ob_ctx, og_ctx, qk_lat, ob_lat, og_lat):
    b, tc_, _ = qk_ctx.shape
    t = qk_lat.shape[1]
    w = M_HEADS * M_HEAD_DIM
    ncc = tc_ // M_CHUNK
    ncl = t // M_CHUNK

    def ctx_idx(d, c):
        cc = jnp.minimum(c, ncc - 1)
        return jnp.where(d == 0, cc, ncc - 1 - cc)

    def lat_idx(d, c):
        cl = jnp.maximum(c - ncc, 0)
        return jnp.where(d == 0, cl, ncl - 1 - cl)

    def cspec(width, colfn):
        return pl.BlockSpec((None, M_CHUNK, width), lambda bb, d, c: (bb, ctx_idx(d, c), colfn(d)))

    def lspec(width, colfn):
        return pl.BlockSpec((None, M_CHUNK, width), lambda bb, d, c: (bb, lat_idx(d, c), colfn(d)))

    return pl.pallas_call(
        functools.partial(_mlstm_kernel, n_ctx_chunks=ncc),
        grid=(b, 2, ncc + ncl),
        in_specs=[cspec(w, lambda d: 0), cspec(w, lambda d: 1), cspec(w, lambda d: 0), cspec(LANES, lambda d: d),
                  lspec(w, lambda d: 0), lspec(w, lambda d: 1), lspec(w, lambda d: 0), lspec(LANES, lambda d: d)],
        out_specs=pl.BlockSpec((None, None, M_CHUNK, w), lambda bb, d, c: (d, bb, lat_idx(d, c), 0)),
        out_shape=jax.ShapeDtypeStruct((2, b, t, w), F32),
        scratch_shapes=[pltpu.VMEM((M_HEADS, M_HEAD_DIM, M_HEAD_DIM), F32),
                        pltpu.VMEM((M_HEADS, 1, M_HEAD_DIM), F32),
                        pltpu.VMEM((M_HEADS, 1, LANES), F32)],
        compiler_params=_params("arbitrary", "arbitrary", "arbitrary"),
        name="mlstm",
    )(qk_ctx, qk_ctx, ob_ctx, og_ctx, qk_lat, qk_lat, ob_lat, og_lat)


Q_TILE_COL = 8
K_TILE_COL = 16
V_TILE_COL = 24


def _attn_kernel(lam_ref, q_ref, k_ref, v_ref, kc_ref, vc_ref, o_ref, qs_ref, m_ref, acc_ref, *, tk, rc):
    tq = q_ref.shape[0]
    q = q_ref[...]
    lane = lax.broadcasted_iota(jnp.int32, q.shape, 1)
    zero = jnp.zeros_like(q)
    qs_ref[0:tq, :] = jnp.where(lane < A_HALF_DIM, q, zero)
    qs_ref[tq:2 * tq, :] = jnp.where(lane >= A_HALF_DIM, q, zero)

    def block(kb, vb, first):
        for c in range(2 * tq // rc):
            rs = slice(c * rc, (c + 1) * rc)
            s = lax.dot_general(qs_ref[rs, :], kb, NT_DIMS, preferred_element_type=F32)
            smax = jnp.max(s, axis=1, keepdims=True)
            if first:
                m_new = smax
                p = jnp.exp(s - m_new)
                acc_ref[rs, :] = jnp.dot(p.astype(BF16), vb, preferred_element_type=F32)
            else:
                m_prev = m_ref[rs, :]
                m_new = jnp.maximum(m_prev, smax)
                alpha = jnp.exp(m_prev - m_new)
                p = jnp.exp(s - m_new)
                acc_ref[rs, :] = alpha * acc_ref[rs, :] + jnp.dot(p.astype(BF16), vb, preferred_element_type=F32)
            m_ref[rs, :] = m_new

    block(kc_ref[...], vc_ref[...], True)

    def body(i, carry):
        off = pl.multiple_of(i * tk, tk)
        block(k_ref[pl.ds(off, tk), :], v_ref[pl.ds(off, tk), :], False)
        return carry

    lax.fori_loop(0, k_ref.shape[0] // tk, body, 0)
    o = acc_ref[:, 0:LANES] / acc_ref[:, LANES:LANES + 1]
    o_ref[...] = o[0:tq] - lam_ref[0, 0] * o[tq:2 * tq]


ATTN_TQ = 512
ATTN_TK = 4096
ATTN_RC = 128


def _attn(lam, ob_lat, ob_ctx, vx_lat, vx_ctx, tq, tk, rc):
    b, t, _ = ob_lat.shape
    tc_ = ob_ctx.shape[1]
    w = A_HEADS * 2 * A_HALF_DIM
    return pl.pallas_call(
        functools.partial(_attn_kernel, tk=tk, rc=rc),
        grid=(b, A_HEADS, t // tq),
        in_specs=[
            pl.BlockSpec(memory_space=pltpu.SMEM),
            pl.BlockSpec((None, tq, LANES), lambda bb, h, i: (bb, i, Q_TILE_COL + h)),
            pl.BlockSpec((None, t, LANES), lambda bb, h, i: (bb, 0, K_TILE_COL + h)),
            pl.BlockSpec((None, t, 2 * LANES), lambda bb, h, i: (bb, 0, h)),
            pl.BlockSpec((None, tc_, LANES), lambda bb, h, i: (bb, 0, K_TILE_COL + h)),
            pl.BlockSpec((None, tc_, 2 * LANES), lambda bb, h, i: (bb, 0, h)),
        ],
        out_specs=pl.BlockSpec((None, tq, LANES), lambda bb, h, i: (bb, i, h)),
        out_shape=jax.ShapeDtypeStruct((b, t, w), F32),
        scratch_shapes=[pltpu.VMEM((2 * tq, LANES), BF16),
                        pltpu.VMEM((2 * tq, 1), F32),
                        pltpu.VMEM((2 * tq, 2 * LANES), F32)],
        compiler_params=_params("arbitrary", "arbitrary", "arbitrary"),
        name="attn",
    )(lam, ob_lat, ob_lat, vx_lat, ob_ctx, vx_ctx)


GA_TILE = 0
GB_TILE = 1
ZM_TILE = 6


def _head_rmsnorm(x, w, n_heads, dh):
    outs = []
    for h in range(n_heads):
        blk = x[:, h * dh:(h + 1) * dh]
        outs.append(blk * lax.rsqrt(jnp.mean(blk * blk, axis=-1, keepdims=True) + EPS))
    return jnp.concatenate(outs, axis=1) * w


def _merge_kernel(hf_ref, hb_ref, z_ref, o_ref, ga_ref, gb_ref, mnw_ref, anw_ref, wpa_ref, wpb_ref, out_ref,
                  *, d_scale):
    hsum = hf_ref[...] + hb_ref[...]
    ym = _sigmoid(z_ref[...]) * _head_rmsnorm(hsum, mnw_ref[...], M_HEADS, M_HEAD_DIM)
    yd = _head_rmsnorm(o_ref[...], anw_ref[...], A_HEADS, 2 * A_HALF_DIM) * d_scale
    pa = jnp.dot(ym.astype(BF16), wpa_ref[...], preferred_element_type=F32)
    pb = jnp.dot(yd.astype(BF16), wpb_ref[...], preferred_element_type=F32)
    out_ref[...] = (_sigmoid(ga_ref[...]) * pa + _sigmoid(gb_ref[...]) * pb).astype(BF16)


def _merge(hdir, of_lat, o_attn, mnw, anw, w_pa, w_pb, d_scale, tm):
    rows = of_lat.shape[0]
    w = M_HEADS * M_HEAD_DIM
    d = w_pa.shape[1]
    return pl.pallas_call(
        functools.partial(_merge_kernel, d_scale=d_scale),
        grid=(rows // tm,),
        in_specs=[
            pl.BlockSpec((None, tm, w), lambda i: (0, i, 0)),
            pl.BlockSpec((None, tm, w), lambda i: (1, i, 0)),
            pl.BlockSpec((tm, w), lambda i: (i, ZM_TILE)),
            pl.BlockSpec((tm, w), lambda i: (i, 0)),
            pl.BlockSpec((tm, d), lambda i: (i, GA_TILE)),
            pl.BlockSpec((tm, d), lambda i: (i, GB_TILE)),
            pl.BlockSpec((1, w), lambda i: (0, 0)),
            pl.BlockSpec((1, w), lambda i: (0, 0)),
            pl.BlockSpec((w, d), lambda i: (0, 0)),
            pl.BlockSpec((w, d), lambda i: (0, 0)),
        ],
        out_specs=pl.BlockSpec((tm, d), lambda i: (i, 0)),
        out_shape=jax.ShapeDtypeStruct((rows, d), BF16),
        compiler_params=_params("arbitrary"),
        name="merge",
    )(hdir, hdir, of_lat, o_attn, of_lat, of_lat, mnw, anw, w_pa, w_pb)


def _resid_kernel(mg_ref, wout_ref, x_ref, g1_ref, n2_ref, sh_ref, sc_ref, x1_ref, h2_ref):
    y = jnp.dot(mg_ref[...], wout_ref[...], preferred_element_type=F32)
    x1 = x_ref[...] + g1_ref[...] * y
    x1_ref[...] = x1
    hn = x1 * lax.rsqrt(jnp.mean(x1 * x1, axis=-1, keepdims=True) + EPS) * n2_ref[...]
    h2_ref[...] = (hn * (1.0 + sc_ref[...]) + sh_ref[...]).astype(BF16)


def _resid(merged, w_out, x2, mod3, norm2_w, blocks_per_batch, tm):
    rows, d = x2.shape

    def mspec(col):
        return pl.BlockSpec((None, 1, d), lambda i: (i // blocks_per_batch, 0, col))

    return pl.pallas_call(
        _resid_kernel,
        grid=(rows // tm,),
        in_specs=[
            pl.BlockSpec((tm, d), lambda i: (i, 0)),
            pl.BlockSpec((d, d), lambda i: (0, 0)),
            pl.BlockSpec((tm, d), lambda i: (i, 0)),
            mspec(2),
            pl.BlockSpec((1, d), lambda i: (0, 0)),
            mspec(3),
            mspec(4),
        ],
        out_specs=[pl.BlockSpec((tm, d), lambda i: (i, 0)), pl.BlockSpec((tm, d), lambda i: (i, 0))],
        out_shape=[jax.ShapeDtypeStruct((rows, d), F32), jax.ShapeDtypeStruct((rows, d), BF16)],
        compiler_params=_params("arbitrary"),
        name="resid",
    )(merged, w_out, x2, mod3, norm2_w, mod3, mod3)


N_TOP = P_TOPK + 1
TOP_ROWS = 24
CAND_COUNTS = tuple(N_TOP // r for r in range(1, N_TOP + 1))
N_CAND = sum(CAND_COUNTS)
CAND_ROWS = -(-N_CAND // SUBLANES) * SUBLANES


def _extract_top(s, count, store):
    rows = s.shape[0]
    ridx = lax.broadcasted_iota(jnp.int32, s.shape, 0)
    for r in range(count):
        m = jnp.max(s, axis=0, keepdims=True)
        first = jnp.min(jnp.where(s == m, ridx, rows), axis=0, keepdims=True)
        s = jnp.where(ridx == first, -jnp.inf, s)
        store(r, m)


def _psel_kernel(h_ref, wpq_ref, keys_ref, s2_ref, e2_ref, thr_ref, e1_ref, q_scr, top_scr, cand_scr, ctop_scr):
    nk = keys_ref.shape[1]
    q = jnp.dot(h_ref[...], wpq_ref[...], preferred_element_type=F32).astype(BF16)
    for hp in range(2 * P_HEADS):
        q_scr[hp] = q[:, hp * LANES:(hp + 1) * LANES]
    cand_scr[...] = jnp.full(cand_scr.shape, -jnp.inf, F32)
    top_scr[...] = jnp.zeros(top_scr.shape, F32)
    ctop_scr[...] = jnp.zeros(ctop_scr.shape, F32)

    def head(hh, carry):
        s_half = []
        for p in range(2):
            s = lax.dot_general(keys_ref[2 * hh + p], q_scr[2 * hh + p], NT_DIMS, preferred_element_type=F32)
            s_half.append(s)

            def store_top(r, row, p=p):
                top_scr[p, pl.ds(r, 1), :] = row

            _extract_top(s, N_TOP, store_top)
        off = 0
        for r1, cnt in enumerate(CAND_COUNTS):
            cand_scr[pl.ds(off, cnt), :] = top_scr[0, pl.ds(r1, 1), :] + top_scr[1, pl.ds(0, cnt), :]
            off += cnt

        def store_ctop(r, row):
            ctop_scr[pl.ds(r, 1), :] = row

        _extract_top(cand_scr[...], N_TOP, store_ctop)
        ctop = ctop_scr[...]
        c_max = ctop[0:1, :]
        z = jnp.sum(jnp.exp(ctop[0:P_TOPK, :] - c_max), axis=0, keepdims=True)
        tau = 0.5 * (ctop[P_TOPK - 1:P_TOPK, :] + ctop[P_TOPK:P_TOPK + 1, :])
        s1, s2 = s_half
        s2_ref[hh] = s2
        e2_ref[hh] = jnp.exp(s2 - top_scr[1, pl.ds(0, 1), :])
        thr_ref[hh] = tau - s1
        e1_ref[hh] = jnp.exp(s1 - top_scr[0, pl.ds(0, 1), :]) / z
        return carry

    lax.fori_loop(0, P_HEADS, head, 0)


def _psel(h2, w_pq, keys, tm):
    rows, d = h2.shape
    nk = keys.shape[1]
    nq = w_pq.shape[1]
    shp = jax.ShapeDtypeStruct((P_HEADS, nk, rows), F32)
    ospec = pl.BlockSpec((P_HEADS, nk, tm), lambda i: (0, 0, i))
    return pl.pallas_call(
        _psel_kernel,
        grid=(rows // tm,),
        in_specs=[
            pl.BlockSpec((tm, d), lambda i: (i, 0)),
            pl.BlockSpec((d, nq), lambda i: (0, 0)),
            pl.BlockSpec((2 * P_HEADS, nk, LANES), lambda i: (0, 0, 0)),
        ],
        out_specs=[ospec, ospec, ospec, ospec],
        out_shape=[shp, shp, shp, shp],
        scratch_shapes=[pltpu.VMEM((2 * P_HEADS, tm, LANES), BF16),
                        pltpu.VMEM((2, TOP_ROWS, tm), F32),
                        pltpu.VMEM((CAND_ROWS, tm), F32),
                        pltpu.VMEM((TOP_ROWS, tm), F32)],
        compiler_params=_params("arbitrary"),
        name="psel",
    )(h2, w_pq, keys)


def _peer_kernel(h_ref, u_ref, vt_ref, s2_ref, e2_ref, thr_ref, e1_ref, o_ref, w0_scr, w1_scr, *, te, tc, n_tiles):
    k = pl.program_id(1)
    nk = s2_ref.shape[1]
    tm = h_ref.shape[0]
    nch = tm // tc

    def gate_chunk(c, w_ref):
        pre = lax.dot_general(u_ref[...], h_ref[c * tc:(c + 1) * tc, :], NT_DIMS,
                              preferred_element_type=F32)
        for ab in range(te // nk):
            a = k * (te // nk) + ab
            thr_rows = [thr_ref[hh, pl.ds(a, 1), :] for hh in range(P_HEADS)]
            e1_rows = [e1_ref[hh, pl.ds(a, 1), :] for hh in range(P_HEADS)]
            for tg in range(tc // LANES):
                ls = slice(c * tc + tg * LANES, c * tc + (tg + 1) * LANES)
                thr = [r[:, ls] for r in thr_rows]
                e1 = [r[:, ls] for r in e1_rows]
                for r0 in range(0, nk, PEER_SUB):
                    g = jnp.zeros((PEER_SUB, LANES), F32)
                    for hh in range(P_HEADS):
                        g = g + jnp.where(s2_ref[hh, r0:r0 + PEER_SUB, ls] >= thr[hh],
                                          e2_ref[hh, r0:r0 + PEER_SUB, ls], 0.0) * e1[hh]
                    x = pre[ab * nk + r0:ab * nk + r0 + PEER_SUB, tg * LANES:(tg + 1) * LANES]
                    act = 0.5 * x * (1.0 + lax.erf(x * np.float32(math.sqrt(0.5))))
                    w_ref[ab * nk + r0:ab * nk + r0 + PEER_SUB, ls] = (g * act).astype(BF16)

    def flush_chunk(c, w_ref, first):
        cols = slice(c * tc, (c + 1) * tc)
        upd = jnp.dot(vt_ref[...], w_ref[:, cols], preferred_element_type=F32)
        o_ref[:, cols] = upd if first else o_ref[:, cols] + upd

    def run(w_new, w_old, first_flush=False):
        for c in range(nch):
            if w_new is not None:
                gate_chunk(c, w_new)
            if w_old is not None:
                flush_chunk(c, w_old, first_flush)

    w = (w0_scr, w1_scr)
    parity = k % 2

    @pl.when(k == 0)
    def _():
        run(w[0], None)

    @pl.when(k == 1)
    def _():
        run(w[1], w[0], first_flush=True)

    for par in range(2):
        @pl.when(jnp.logical_and(jnp.logical_and(k > 1, k < n_tiles), parity == par))
        def _(par=par):
            run(w[par], w[1 - par])

    @pl.when(k == n_tiles)
    def _():
        run(None, w[(n_tiles - 1) % 2])


PEER_TE = 512
PEER_TC = 256
PEER_SUB = 64


def _peer(h2, u_bf, vt_bf, s2, e2, thr, e1, tm, te, tc):
    rows, d = h2.shape
    ne = u_bf.shape[0]
    nk = s2.shape[1]
    n_tiles = ne // te
    assert n_tiles >= 2
    sspec = pl.BlockSpec((P_HEADS, nk, tm), lambda i, k: (0, 0, i))
    return pl.pallas_call(
        functools.partial(_peer_kernel, te=te, tc=tc, n_tiles=n_tiles),
        grid=(rows // tm, n_tiles + 1),
        in_specs=[
            pl.BlockSpec((tm, d), lambda i, k: (i, 0)),
            pl.BlockSpec((te, d), lambda i, k: (jnp.minimum(k, n_tiles - 1), 0)),
            pl.BlockSpec((d, te), lambda i, k: (0, jnp.maximum(k - 1, 0))),
            sspec, sspec, sspec, sspec,
        ],
        out_specs=pl.BlockSpec((d, tm), lambda i, k: (0, i)),
        out_shape=jax.ShapeDtypeStruct((d, rows), F32),
        scratch_shapes=[pltpu.VMEM((te, tm), BF16), pltpu.VMEM((te, tm), BF16)],
        compiler_params=_params("arbitrary", "arbitrary"),
        name="peer",
    )(h2, u_bf, vt_bf, s2, e2, thr, e1)


def _final_kernel(x1_ref, pt_ref, g2_ref, fw_ref, o_ref):
    x2 = x1_ref[...] + g2_ref[...] * pt_ref[...].T
    o_ref[...] = x2 * lax.rsqrt(jnp.mean(x2 * x2, axis=-1, keepdims=True) + EPS) * fw_ref[...]


def _final(x1, peer_t, mod3, final_w, blocks_per_batch, tm):
    rows, d = x1.shape
    return pl.pallas_call(
        _final_kernel,
        grid=(rows // tm,),
        in_specs=[
            pl.BlockSpec((tm, d), lambda i: (i, 0)),
            pl.BlockSpec((d, tm), lambda i: (0, i)),
            pl.BlockSpec((None, 1, d), lambda i: (i // blocks_per_batch, 0, 5)),
            pl.BlockSpec((1, d), lambda i: (0, 0)),
        ],
        out_specs=pl.BlockSpec((tm, d), lambda i: (i, 0)),
        out_shape=jax.ShapeDtypeStruct((rows, d), F32),
        compiler_params=_params("arbitrary"),
        name="final",
    )(x1, peer_t, mod3, final_w)


def _rope_tables(t):
    rows = t // GRID_W
    row = jnp.repeat(jnp.arange(rows, dtype=F32), GRID_W)
    col = jnp.tile(jnp.arange(GRID_W, dtype=F32), rows)
    axis_dim = A_HALF_DIM // 2
    inv_freq = ROPE_BASE ** (-jnp.arange(0, axis_dim, 2, dtype=F32) / axis_dim)
    ang = jnp.concatenate([row[:, None] * inv_freq, col[:, None] * inv_freq], axis=-1)
    cos = jnp.tile(jnp.cos(ang), (1, LANES // axis_dim))
    sin = jnp.tile(jnp.concatenate([-jnp.sin(ang), jnp.sin(ang)], axis=-1), (1, LANES // (2 * axis_dim)))
    return cos, sin


def _v_ext(ob3):
    b, t, _ = ob3.shape
    dv = 2 * A_HALF_DIM
    v = ob3[:, :, V_TILE_COL * LANES:].reshape(b, t, A_HEADS, dv)
    ext = jnp.concatenate([jnp.ones((b, t, A_HEADS, 1), BF16), jnp.zeros((b, t, A_HEADS, LANES - 1), BF16)], axis=-1)
    return jnp.concatenate([v, ext], axis=-1).reshape(b, t, A_HEADS * 2 * LANES)


def kernel(x, c, ctx, c_ctx, w_mod, b_mod, norm1_w, w_in, b_in, conv_w, conv_b, m_norm_w, lambdas, a_norm_w,
           w_pa, w_pb, w_out, norm2_w, w_pq, sub_keys, expert_u, expert_v, final_norm_w):
    depth = w_mod.shape[0]
    assert depth == 1, "single-layer configuration only (context outputs never reach a latent token)"
    bsz, t, d = x.shape
    t_ctx = ctx.shape[1]
    mw = M_HEADS * M_HEAD_DIM
    aw = A_HEADS * 2 * A_HALF_DIM
    assert w_in.shape[2] == 4 * mw + 4 * M_HEADS + 3 * aw + 2 * d and d == 2 * TILE_N and mw == TILE_N == aw
    lam_init = 0.8 - 0.6 * math.exp(-0.3 * 0)

    mod_rows = -(-(bsz + 1) // SUBLANES) * SUBLANES
    cvec = jnp.zeros((mod_rows, d), F32).at[:bsz].set(c).at[bsz].set(c_ctx)
    mod3 = _mod(cvec, w_mod[0], b_mod[0]).reshape(mod_rows, 1, 6 * d)

    wi, bi = w_in[0], b_in[0]
    offs = np.cumsum((0, mw, mw, mw, mw, 4 * M_HEADS, aw, aw, aw, d, d))
    seg = {n: slice(int(offs[i]), int(offs[i + 1]))
           for i, n in enumerate(("qm", "km", "vm", "zm", "g", "qa", "ka", "va", "ga", "gb"))}
    order = ("ga", "gb", "qm", "km", "zm", "vm", "qa", "ka", "va")
    w_main = jnp.concatenate([wi[:, seg[n]] for n in order], axis=1).astype(BF16)
    b_main = jnp.concatenate([bi[seg[n]] for n in order]).reshape(1, -1)
    gpad = LANES - 2 * M_HEADS
    wg, bg = wi[:, seg["g"]], bi[seg["g"]]
    w_g = jnp.concatenate([wg[:, :2 * M_HEADS], jnp.zeros((d, gpad), F32),
                           wg[:, 2 * M_HEADS:], jnp.zeros((d, gpad), F32)], axis=1).astype(BF16)
    b_g = jnp.concatenate([bg[:2 * M_HEADS], jnp.zeros((gpad,), F32),
                           bg[2 * M_HEADS:], jnp.zeros((gpad,), F32)]).reshape(1, -1)

    cos, sin = _rope_tables(t)
    cos_l = jnp.tile(cos, (bsz, 1))
    sin_l = jnp.tile(sin, (bsz, 1))
    n1 = norm1_w[0].reshape(1, d)

    tm_in = 512
    blocks_per_batch = t // tm_in
    of_lat, ob_lat, og_lat = _inproj(x.reshape(bsz * t, d), mod3, lambda i: i // blocks_per_batch, n1,
                                     w_main, b_main, w_g, b_g, cos_l, sin_l, tm_in)
    rows_c = bsz * t_ctx
    of_ctx, ob_ctx, og_ctx = _inproj(ctx.reshape(rows_c, d), mod3, lambda i: bsz, n1, w_main, b_main, w_g, b_g,
                                     jnp.ones((rows_c, LANES), F32), jnp.zeros((rows_c, LANES), F32), rows_c)

    qk_lat = _conv(of_lat.reshape(bsz, t, -1), conv_w[0], conv_b[0], 512)
    qk_ctx = _conv(of_ctx.reshape(bsz, t_ctx, -1), conv_w[0], conv_b[0], t_ctx)
    ob_lat3 = ob_lat.reshape(bsz, t, -1)
    ob_ctx3 = ob_ctx.reshape(bsz, t_ctx, -1)
    hdir = _mlstm(qk_ctx, ob_ctx3, og_ctx.reshape(bsz, t_ctx, -1), qk_lat, ob_lat3, og_lat.reshape(bsz, t, -1))

    lq1, lk1, lq2, lk2 = lambdas[0].astype(F32)
    lam = (jnp.exp(jnp.sum(lq1 * lk1)) - jnp.exp(jnp.sum(lq2 * lk2)) + lam_init).reshape(1, 1)
    o_attn = _attn(lam, ob_lat3, ob_ctx3, _v_ext(ob_lat3), _v_ext(ob_ctx3), ATTN_TQ, min(ATTN_TK, t), ATTN_RC)

    merged = _merge(hdir.reshape(2, bsz * t, mw), of_lat, o_attn.reshape(bsz * t, aw),
                    m_norm_w[0].reshape(1, mw), jnp.tile(a_norm_w[0], A_HEADS).reshape(1, aw),
                    w_pa[0].astype(BF16), w_pb[0].astype(BF16), 1.0 - lam_init, 256)
    tm_r = 256
    x1, h2 = _resid(merged, w_out[0].astype(BF16), x.reshape(bsz * t, d), mod3, norm2_w[0].reshape(1, d),
                    t // tm_r, tm_r)

    nk = sub_keys.shape[3]
    keys = sub_keys[0].reshape(2 * P_HEADS, nk, sub_keys.shape[4]).astype(BF16)
    s2, e2, thr, e1 = _psel(h2, w_pq[0].astype(BF16), keys, 256)
    tm_p = 512
    peer_t = _peer(h2, expert_u[0].astype(BF16), expert_v[0].T.astype(BF16), s2, e2, thr, e1, tm_p, PEER_TE, PEER_TC)

    tm_f = 256
    out = _final(x1, peer_t, mod3, final_norm_w.reshape(1, d), t // tm_f, tm_f)
    return out.reshape(bsz, t, d)
```

```python
import functools
import math

import jax
import jax.numpy as jnp
import numpy as np
from jax import lax
from jax.experimental import pallas as pl
from jax.experimental.pallas import tpu as pltpu

F32 = jnp.float32
BF16 = jnp.bfloat16

EPS = 1e-6
GRID_W = 64
ROPE_BASE = 10000.0
M_HEADS = 8
M_HEAD_DIM = 128
M_CHUNK = 128
CONV_K = 5
A_HEADS = 8
A_HALF_DIM = 64
P_HEADS = 8
P_TOPK = 16
LANES = 128
SUBLANES = 8
VMEM_LIMIT = 56 * 1024 * 1024

NT_DIMS = (((1,), (1,)), ((), ()))
TN_DIMS = (((0,), (0,)), ((), ()))


def _params(*sem):
    return pltpu.CompilerParams(dimension_semantics=sem, vmem_limit_bytes=VMEM_LIMIT)


def _sigmoid(x):
    return 1.0 / (1.0 + jnp.exp(-x))


def _mod_kernel(c_ref, w_ref, b_ref, o_ref):
    c = c_ref[...]
    s = (c * _sigmoid(c)).astype(BF16)
    o_ref[...] = jnp.dot(s, w_ref[...].astype(BF16), preferred_element_type=F32) + b_ref[...]


def _mod(cvec, w_mod, b_mod):
    rows, d = cvec.shape
    n = w_mod.shape[1]
    tn = 1024
    return pl.pallas_call(
        _mod_kernel,
        grid=(n // tn,),
        in_specs=[pl.BlockSpec((rows, d), lambda j: (0, 0)),
                  pl.BlockSpec((d, tn), lambda j: (0, j)),
                  pl.BlockSpec((1, tn), lambda j: (0, j))],
        out_specs=pl.BlockSpec((rows, tn), lambda j: (0, j)),
        out_shape=jax.ShapeDtypeStruct((rows, n), F32),
        compiler_params=_params("arbitrary"),
        name="mod",
    )(cvec, w_mod, b_mod.reshape(1, n))


N_F32_TILES = 7
N_BF16_TILES = 4
TILE_N = 1024


def _inproj_kernel(x_ref, n1_ref, sh_ref, sc_ref, w_ref, b_ref, wg_ref, bg_ref, cos_ref, sin_ref,
                   of_ref, ob_ref, og_ref, h_ref):
    j = pl.program_id(1)

    @pl.when(j == 0)
    def _():
        x = x_ref[...]
        y = x * lax.rsqrt(jnp.mean(x * x, axis=-1, keepdims=True) + EPS) * n1_ref[...]
        h = (y * (1.0 + sc_ref[...]) + sh_ref[...]).astype(BF16)
        h_ref[...] = h
        og_ref[...] = jnp.dot(h, wg_ref[...], preferred_element_type=F32) + bg_ref[...]

    acc = jnp.dot(h_ref[...], w_ref[...], preferred_element_type=F32) + b_ref[...]

    @pl.when(j < N_F32_TILES)
    def _():
        of_ref[...] = acc

    @pl.when(jnp.logical_or(j == N_F32_TILES, j == N_F32_TILES + 3))
    def _():
        ob_ref[...] = acc.astype(BF16)

    def rope(scale):
        cos = cos_ref[...]
        sin = sin_ref[...]
        lane = lax.broadcasted_iota(jnp.int32, cos.shape, 1)
        first = (lane % (2 * (A_HALF_DIM // 2))) < (A_HALF_DIM // 2)
        for hh in range(TILE_N // LANES):
            a = acc[:, hh * LANES:(hh + 1) * LANES]
            r = jnp.where(first, pltpu.roll(a, LANES - A_HALF_DIM // 2, 1), pltpu.roll(a, A_HALF_DIM // 2, 1))
            ob_ref[:, hh * LANES:(hh + 1) * LANES] = ((a * cos + r * sin) * scale).astype(BF16)

    @pl.when(j == N_F32_TILES + 1)
    def _():
        rope(A_HALF_DIM ** -0.5)

    @pl.when(j == N_F32_TILES + 2)
    def _():
        rope(1.0)


def _inproj(x2, mod3, seg_of_block, norm_w, w_main, b_main, w_g, b_g, cos, sin, tm):
    rows, d = x2.shape
    nj = N_F32_TILES + N_BF16_TILES
    grid = (rows // tm, nj)
    return pl.pallas_call(
        _inproj_kernel,
        grid=grid,
        in_specs=[
            pl.BlockSpec((tm, d), lambda i, j: (i, 0)),
            pl.BlockSpec((1, d), lambda i, j: (0, 0)),
            pl.BlockSpec((None, 1, d), lambda i, j: (seg_of_block(i), 0, 0)),
            pl.BlockSpec((None, 1, d), lambda i, j: (seg_of_block(i), 0, 1)),
            pl.BlockSpec((d, TILE_N), lambda i, j: (0, j)),
            pl.BlockSpec((1, TILE_N), lambda i, j: (0, j)),
            pl.BlockSpec((d, 2 * LANES), lambda i, j: (0, 0)),
            pl.BlockSpec((1, 2 * LANES), lambda i, j: (0, 0)),
            pl.BlockSpec((tm, LANES), lambda i, j: (i, 0)),
            pl.BlockSpec((tm, LANES), lambda i, j: (i, 0)),
        ],
        out_specs=[
            pl.BlockSpec((tm, TILE_N), lambda i, j: (i, jnp.minimum(j, N_F32_TILES - 1))),
            pl.BlockSpec((tm, TILE_N), lambda i, j: (i, jnp.maximum(j - N_F32_TILES, 0))),
            pl.BlockSpec((tm, 2 * LANES), lambda i, j: (i, 0)),
        ],
        out_shape=[
            jax.ShapeDtypeStruct((rows, N_F32_TILES * TILE_N), F32),
            jax.ShapeDtypeStruct((rows, N_BF16_TILES * TILE_N), BF16),
            jax.ShapeDtypeStruct((rows, 2 * LANES), F32),
        ],
        scratch_shapes=[pltpu.VMEM((tm, d), BF16)],
        compiler_params=_params("arbitrary", "arbitrary"),
        name="inproj",
    )(x2, norm_w, mod3, mod3, w_main, b_main, w_g, b_g, cos, sin)


QM_TILE = 4


def _conv_kernel(prev_ref, cur_ref, next_ref, w_ref, b_ref, o_ref, buf_ref, *, q_scale):
    i = pl.program_id(1)
    n = pl.num_programs(1)
    jc = pl.program_id(2)
    tc = cur_ref.shape[0]
    halo = SUBLANES
    buf_ref[0:halo, :] = jnp.where(i > 0, prev_ref[...], 0.0)
    buf_ref[halo:halo + tc, :] = cur_ref[...]
    buf_ref[halo + tc:2 * halo + tc, :] = jnp.where(i < n - 1, next_ref[...], 0.0)
    acc = jnp.zeros(cur_ref.shape, F32) + b_ref[...]
    for k in range(CONV_K):
        acc = acc + buf_ref[pl.ds(halo + k - CONV_K // 2, tc), :] * w_ref[k:k + 1, :]
    y = acc * _sigmoid(acc)
    y = y * jnp.where(jc == 0, q_scale, 1.0)
    o_ref[...] = y.astype(BF16)


def _conv(of3, conv_w, conv_b, tc):
    b, t, ncol = of3.shape
    w = M_HEADS * M_HEAD_DIM
    of4 = of3.reshape(b, t // SUBLANES, SUBLANES, ncol)
    g = tc // SUBLANES
    last = t // SUBLANES - 1
    return pl.pallas_call(
        functools.partial(_conv_kernel, q_scale=M_HEAD_DIM ** -0.5),
        grid=(b, t // tc, 2),
        in_specs=[
            pl.BlockSpec((None, None, SUBLANES, w), lambda bb, i, jc: (bb, jnp.maximum(i * g - 1, 0), 0, QM_TILE + jc)),
            pl.BlockSpec((None, tc, w), lambda bb, i, jc: (bb, i, QM_TILE + jc)),
            pl.BlockSpec((None, None, SUBLANES, w), lambda bb, i, jc: (bb, jnp.minimum((i + 1) * g, last), 0, QM_TILE + jc)),
            pl.BlockSpec((CONV_K, w), lambda bb, i, jc: (0, jc)),
            pl.BlockSpec((1, w), lambda bb, i, jc: (0, jc)),
        ],
        out_specs=pl.BlockSpec((None, tc, w), lambda bb, i, jc: (bb, i, jc)),
        out_shape=jax.ShapeDtypeStruct((b, t, 2 * w), BF16),
        scratch_shapes=[pltpu.VMEM((tc + 2 * SUBLANES, w), F32)],
        compiler_params=_params("arbitrary", "arbitrary", "arbitrary"),
        name="conv",
    )(of4, of3, of4, conv_w, conv_b.reshape(1, 2 * w))


def _split3(x):
    hi = x.astype(BF16)
    r = x - hi.astype(F32)
    mid = r.astype(BF16)
    lo = (r - mid.astype(F32)).astype(BF16)
    return hi, mid, lo


def _log_sigmoid(x):
    return jnp.minimum(x, 0.0) - jnp.log1p(jnp.exp(-jnp.abs(x)))


def _mlstm_kernel(qc_ref, kc_ref, vc_ref, gc_ref, ql_ref, kl_ref, vl_ref, gl_ref, o_ref,
                  c_ref, n_ref, m_ref, *, n_ctx_chunks):
    d = pl.program_id(1)
    c = pl.program_id(2)
    L = M_CHUNK
    dh = M_HEAD_DIM

    @pl.when(c == 0)
    def _():
        c_ref[...] = jnp.zeros(c_ref.shape, F32)
        n_ref[...] = jnp.zeros(n_ref.shape, F32)
        m_ref[...] = jnp.zeros(m_ref.shape, F32)

    def step(q_ref, k_ref, v_ref, g_ref, with_output):
        row = lax.broadcasted_iota(jnp.int32, (L, L), 0)
        col = lax.broadcasted_iota(jnp.int32, (L, L), 1)
        causal = (row - col) * (1 - 2 * d) >= 0
        tri = causal.astype(BF16)
        g = g_ref[...]
        lf = _log_sigmoid(g)
        hi, mid, lo = _split3(lf)
        b = (jnp.dot(tri, hi, preferred_element_type=F32) + jnp.dot(tri, mid, preferred_element_type=F32)
             + jnp.dot(tri, lo, preferred_element_type=F32))
        btot = jnp.sum(lf, axis=0, keepdims=True)
        b_t = b.T
        g_t = g.T
        for h in range(M_HEADS):
            hs = slice(h * dh, (h + 1) * dh)
            q = q_ref[:, hs]
            k = k_ref[:, hs]
            v = v_ref[:, hs]
            i_c = g[:, h:h + 1]
            b_c = b[:, M_HEADS + h:M_HEADS + h + 1]
            i_r = g_t[h:h + 1, :]
            b_r = b_t[M_HEADS + h:M_HEADS + h + 1, :]
            b_end = btot[:, M_HEADS + h:M_HEADS + h + 1]
            m_prev = m_ref[h][:, 0:1]
            if with_output:
                log_d = jnp.where(causal, b_c - b_r + i_r, -jnp.inf)
                m_t = jnp.maximum(b_c + m_prev, jnp.max(log_d, axis=1, keepdims=True))
                inter = jnp.exp(b_c + m_prev - m_t)
                dmat = jnp.exp(log_d - m_t)
                s = lax.dot_general(q, k, NT_DIMS, preferred_element_type=F32)
                qk = s * dmat
                num = (inter * jnp.dot(q, c_ref[h].astype(BF16), preferred_element_type=F32)
                       + jnp.dot(qk.astype(BF16), v, preferred_element_type=F32))
                qn = jnp.sum(q.astype(F32) * n_ref[h], axis=1, keepdims=True)
                den = inter * qn + jnp.sum(qk, axis=1, keepdims=True)
                o_ref[:, hs] = num / jnp.maximum(jnp.abs(den), jnp.exp(-m_t))
            lw_c = b_end - b_c + i_c
            lw_r = b_end - b_r + i_r
            m_end = jnp.maximum(b_end + m_prev, jnp.max(lw_r, axis=1, keepdims=True))
            w_c = jnp.exp(lw_c - m_end)
            decay = jnp.exp(b_end + m_prev - m_end)
            kw = k.astype(F32) * w_c
            c_ref[h] = decay * c_ref[h] + lax.dot_general(kw.astype(BF16), v, TN_DIMS, preferred_element_type=F32)
            n_ref[h] = decay * n_ref[h] + jnp.sum(kw, axis=0, keepdims=True)
            m_ref[h] = jnp.broadcast_to(m_end, (1, LANES))

    @pl.when(c < n_ctx_chunks)
    def _():
        step(qc_ref, kc_ref, vc_ref, gc_ref, False)

    @pl.when(c >= n_ctx_chunks)
    def _():
        step(ql_ref, kl_ref, vl_ref, gl_ref, True)


def _mlstm(qk_ctx, ob_ctx, og_ctx, qk_lat, ob_lat, og_lat):
    b, tc_, _ = qk_ctx.shape
    t = qk_lat.shape[1]
    w = M_HEADS * M_HEAD_DIM
    ncc = tc_ // M_CHUNK
    ncl = t // M_CHUNK

    def ctx_idx(d, c):
        cc = jnp.minimum(c, ncc - 1)
        return jnp.where(d == 0, cc, ncc - 1 - cc)

    def lat_idx(d, c):
        cl = jnp.maximum(c - ncc, 0)
        return jnp.where(d == 0, cl, ncl - 1 - cl)

    def cspec(width, colfn):
        return pl.BlockSpec((None, M_CHUNK, width), lambda bb, d, c: (bb, ctx_idx(d, c), colfn(d)))

    def lspec(width, colfn):
        return pl.BlockSpec((None, M_CHUNK, width), lambda bb, d, c: (bb, lat_idx(d, c), colfn(d)))

    return pl.pallas_call(
        functools.partial(_mlstm_kernel, n_ctx_chunks=ncc),
        grid=(b, 2, ncc + ncl),
        in_specs=[cspec(w, lambda d: 0), cspec(w, lambda d: 1), cspec(w, lambda d: 0), cspec(LANES, lambda d: d),
                  lspec(w, lambda d: 0), lspec(w, lambda d: 1), lspec(w, lambda d: 0), lspec(LANES, lambda d: d)],
        out_specs=pl.BlockSpec((None, None, M_CHUNK, w), lambda bb, d, c: (d, bb, lat_idx(d, c), 0)),
        out_shape=jax.ShapeDtypeStruct((2, b, t, w), F32),
        scratch_shapes=[pltpu.VMEM((M_HEADS, M_HEAD_DIM, M_HEAD_DIM), F32),
                        pltpu.VMEM((M_HEADS, 1, M_HEAD_DIM), F32),
                        pltpu.VMEM((M_HEADS, 1, LANES), F32)],
        compiler_params=_params("arbitrary", "arbitrary", "arbitrary"),
        name="mlstm",
    )(qk_ctx, qk_ctx, ob_ctx, og_ctx, qk_lat, qk_lat, ob_lat, og_lat)


Q_TILE_COL = 8
K_TILE_COL = 16
V_TILE_COL = 24


def _attn_kernel(lam_ref, q_ref, k_ref, v_in_ref, kc_ref, vc_in_ref, o_ref, qs_ref, m_ref, acc_ref, v_ref, vc_ref,
                 *, tk, rc):
    tq = q_ref.shape[0]

    @pl.when(pl.program_id(2) == 0)
    def _():
        for src, dst in ((v_in_ref, v_ref), (vc_in_ref, vc_ref)):
            lane = lax.broadcasted_iota(jnp.int32, src.shape, 1)
            dst[:, 0:LANES] = src[...]
            dst[:, LANES:2 * LANES] = jnp.where(lane == 0, 1.0, 0.0).astype(BF16)

    q = q_ref[...]
    lane = lax.broadcasted_iota(jnp.int32, q.shape, 1)
    zero = jnp.zeros_like(q)
    qs_ref[0:tq, :] = jnp.where(lane < A_HALF_DIM, q, zero)
    qs_ref[tq:2 * tq, :] = jnp.where(lane >= A_HALF_DIM, q, zero)

    def block(kb, vb, first):
        for c in range(2 * tq // rc):
            rs = slice(c * rc, (c + 1) * rc)
            s = lax.dot_general(qs_ref[rs, :], kb, NT_DIMS, preferred_element_type=F32)
            smax = jnp.max(s, axis=1, keepdims=True)
            if first:
                m_new = smax
                p = jnp.exp(s - m_new)
                acc_ref[rs, :] = jnp.dot(p.astype(BF16), vb, preferred_element_type=F32)
            else:
                m_prev = m_ref[rs, :]
                m_new = jnp.maximum(m_prev, smax)
                alpha = jnp.exp(m_prev - m_new)
                p = jnp.exp(s - m_new)
                acc_ref[rs, :] = alpha * acc_ref[rs, :] + jnp.dot(p.astype(BF16), vb, preferred_element_type=F32)
            m_ref[rs, :] = m_new

    block(kc_ref[...], vc_ref[...], True)

    def body(i, carry):
        off = pl.multiple_of(i * tk, tk)
        block(k_ref[pl.ds(off, tk), :], v_ref[pl.ds(off, tk), :], False)
        return carry

    lax.fori_loop(0, k_ref.shape[0] // tk, body, 0)
    o = acc_ref[:, 0:LANES] / acc_ref[:, LANES:LANES + 1]
    o_ref[...] = o[0:tq] - lam_ref[0, 0] * o[tq:2 * tq]


ATTN_TQ = 1024
ATTN_TK = 4096
ATTN_RC = 128


def _attn(lam, ob_lat, ob_ctx, tq, tk, rc):
    b, t, _ = ob_lat.shape
    tc_ = ob_ctx.shape[1]
    w = A_HEADS * 2 * A_HALF_DIM
    return pl.pallas_call(
        functools.partial(_attn_kernel, tk=tk, rc=rc),
        grid=(b, A_HEADS, t // tq),
        in_specs=[
            pl.BlockSpec(memory_space=pltpu.SMEM),
            pl.BlockSpec((None, tq, LANES), lambda bb, h, i: (bb, i, Q_TILE_COL + h)),
            pl.BlockSpec((None, t, LANES), lambda bb, h, i: (bb, 0, K_TILE_COL + h)),
            pl.BlockSpec((None, t, LANES), lambda bb, h, i: (bb, 0, V_TILE_COL + h)),
            pl.BlockSpec((None, tc_, LANES), lambda bb, h, i: (bb, 0, K_TILE_COL + h)),
            pl.BlockSpec((None, tc_, LANES), lambda bb, h, i: (bb, 0, V_TILE_COL + h)),
        ],
        out_specs=pl.BlockSpec((None, tq, LANES), lambda bb, h, i: (bb, i, h)),
        out_shape=jax.ShapeDtypeStruct((b, t, w), F32),
        scratch_shapes=[pltpu.VMEM((2 * tq, LANES), BF16),
                        pltpu.VMEM((2 * tq, 1), F32),
                        pltpu.VMEM((2 * tq, 2 * LANES), F32),
                        pltpu.VMEM((t, 2 * LANES), BF16),
                        pltpu.VMEM((tc_, 2 * LANES), BF16)],
        compiler_params=_params("arbitrary", "arbitrary", "arbitrary"),
        name="attn",
    )(lam, ob_lat, ob_lat, ob_lat, ob_ctx, ob_ctx)


GA_TILE = 0
GB_TILE = 1
ZM_TILE = 6


def _head_rmsnorm(x, w, n_heads, dh):
    outs = []
    for h in range(n_heads):
        blk = x[:, h * dh:(h + 1) * dh]
        outs.append(blk * lax.rsqrt(jnp.mean(blk * blk, axis=-1, keepdims=True) + EPS))
    return jnp.concatenate(outs, axis=1) * w


def _merge_kernel(hf_ref, hb_ref, z_ref, o_ref, ga_ref, gb_ref, mnw_ref, anw_ref, wpa_ref, wpb_ref, out_ref,
                  *, d_scale):
    hsum = hf_ref[...] + hb_ref[...]
    ym = _sigmoid(z_ref[...]) * _head_rmsnorm(hsum, mnw_ref[...], M_HEADS, M_HEAD_DIM)
    yd = _head_rmsnorm(o_ref[...], anw_ref[...], A_HEADS, 2 * A_HALF_DIM) * d_scale
    pa = jnp.dot(ym.astype(BF16), wpa_ref[...], preferred_element_type=F32)
    pb = jnp.dot(yd.astype(BF16), wpb_ref[...], preferred_element_type=F32)
    out_ref[...] = (_sigmoid(ga_ref[...]) * pa + _sigmoid(gb_ref[...]) * pb).astype(BF16)


def _merge(hdir, of_lat, o_attn, mnw, anw, w_pa, w_pb, d_scale, tm):
    rows = of_lat.shape[0]
    w = M_HEADS * M_HEAD_DIM
    d = w_pa.shape[1]
    return pl.pallas_call(
        functools.partial(_merge_kernel, d_scale=d_scale),
        grid=(rows // tm,),
        in_specs=[
            pl.BlockSpec((None, tm, w), lambda i: (0, i, 0)),
            pl.BlockSpec((None, tm, w), lambda i: (1, i, 0)),
            pl.BlockSpec((tm, w), lambda i: (i, ZM_TILE)),
            pl.BlockSpec((tm, w), lambda i: (i, 0)),
            pl.BlockSpec((tm, d), lambda i: (i, GA_TILE)),
            pl.BlockSpec((tm, d), lambda i: (i, GB_TILE)),
            pl.BlockSpec((1, w), lambda i: (0, 0)),
            pl.BlockSpec((1, w), lambda i: (0, 0)),
            pl.BlockSpec((w, d), lambda i: (0, 0)),
            pl.BlockSpec((w, d), lambda i: (0, 0)),
        ],
        out_specs=pl.BlockSpec((tm, d), lambda i: (i, 0)),
        out_shape=jax.ShapeDtypeStruct((rows, d), BF16),
        compiler_params=_params("arbitrary"),
        name="merge",
    )(hdir, hdir, of_lat, o_attn, of_lat, of_lat, mnw, anw, w_pa, w_pb)


def _resid_kernel(mg_ref, wout_ref, x_ref, g1_ref, n2_ref, sh_ref, sc_ref, x1_ref, h2_ref):
    y = jnp.dot(mg_ref[...], wout_ref[...], preferred_element_type=F32)
    x1 = x_ref[...] + g1_ref[...] * y
    x1_ref[...] = x1
    hn = x1 * lax.rsqrt(jnp.mean(x1 * x1, axis=-1, keepdims=True) + EPS) * n2_ref[...]
    h2_ref[...] = (hn * (1.0 + sc_ref[...]) + sh_ref[...]).astype(BF16)


def _resid(merged, w_out, x2, mod3, norm2_w, blocks_per_batch, tm):
    rows, d = x2.shape

    def mspec(col):
        return pl.BlockSpec((None, 1, d), lambda i: (i // blocks_per_batch, 0, col))

    return pl.pallas_call(
        _resid_kernel,
        grid=(rows // tm,),
        in_specs=[
            pl.BlockSpec((tm, d), lambda i: (i, 0)),
            pl.BlockSpec((d, d), lambda i: (0, 0)),
            pl.BlockSpec((tm, d), lambda i: (i, 0)),
            mspec(2),
            pl.BlockSpec((1, d), lambda i: (0, 0)),
            mspec(3),
            mspec(4),
        ],
        out_specs=[pl.BlockSpec((tm, d), lambda i: (i, 0)), pl.BlockSpec((tm, d), lambda i: (i, 0))],
        out_shape=[jax.ShapeDtypeStruct((rows, d), F32), jax.ShapeDtypeStruct((rows, d), BF16)],
        compiler_params=_params("arbitrary"),
        name="resid",
    )(merged, w_out, x2, mod3, norm2_w, mod3, mod3)


CAND_COUNTS = tuple(P_TOPK // r for r in range(1, P_TOPK + 1))
N_CAND = sum(CAND_COUNTS)
CAND_ROWS = -(-N_CAND // SUBLANES) * SUBLANES
UNRANKED = 255.0


def _extract_top(s, count, store):
    rows = s.shape[0]
    ridx = lax.broadcasted_iota(jnp.int32, s.shape, 0)
    rank = jnp.full(s.shape, UNRANKED, F32)
    for r in range(count):
        m = jnp.max(s, axis=0, keepdims=True)
        first = jnp.min(jnp.where(s == m, ridx, rows), axis=0, keepdims=True)
        hit = ridx == first
        s = jnp.where(hit, -jnp.inf, s)
        rank = jnp.where(hit, float(r), rank)
        store(r, m)
    return rank


def _psel_kernel(h_ref, wpq_ref, keys_ref, rank2_ref, e2_ref, rc_ref, e1_ref, q_scr, top_scr, cand_scr, ctop_scr,
                 csel_scr):
    q = jnp.dot(h_ref[...], wpq_ref[...], preferred_element_type=F32).astype(BF16)
    for hp in range(2 * P_HEADS):
        q_scr[hp] = q[:, hp * LANES:(hp + 1) * LANES]
    cand_scr[...] = jnp.full(cand_scr.shape, -jnp.inf, F32)

    def head(hh, carry):
        s_half, rank_half = [], []
        for p in range(2):
            s = lax.dot_general(keys_ref[2 * hh + p], q_scr[2 * hh + p], NT_DIMS, preferred_element_type=F32)
            s_half.append(s)

            def store_top(r, row, p=p):
                top_scr[p, pl.ds(r, 1), :] = row

            rank_half.append(_extract_top(s, P_TOPK, store_top))
        off = 0
        for r1, cnt in enumerate(CAND_COUNTS):
            cand_scr[pl.ds(off, cnt), :] = top_scr[0, pl.ds(r1, 1), :] + top_scr[1, pl.ds(0, cnt), :]
            off += cnt

        def store_ctop(r, row):
            ctop_scr[pl.ds(r, 1), :] = row

        crank = _extract_top(cand_scr[...], P_TOPK, store_ctop)
        csel_scr[...] = jnp.where(crank < P_TOPK, 1.0, 0.0)
        ctop = ctop_scr[...]
        z = jnp.sum(jnp.exp(ctop - ctop[0:1, :]), axis=0, keepdims=True)
        (s1, s2), (rank1, rank2) = s_half, rank_half
        rc = jnp.zeros(s1.shape, F32)
        off = 0
        for r1, cnt in enumerate(CAND_COUNTS):
            picked = jnp.sum(csel_scr[pl.ds(off, cnt), :], axis=0, keepdims=True)
            rc = jnp.where(rank1 == float(r1), picked, rc)
            off += cnt
        rank2_ref[hh] = rank2
        e2_ref[hh] = jnp.exp(s2 - top_scr[1, pl.ds(0, 1), :])
        rc_ref[hh] = rc
        e1_ref[hh] = 0.5 * jnp.exp(s1 - top_scr[0, pl.ds(0, 1), :]) / z
        return carry

    lax.fori_loop(0, P_HEADS, head, 0)


def _psel(h2, w_pq, keys, tm):
    rows, d = h2.shape
    nk = keys.shape[1]
    nq = w_pq.shape[1]
    shp_f = jax.ShapeDtypeStruct((P_HEADS, nk, rows), F32)
    ospec = pl.BlockSpec((P_HEADS, nk, tm), lambda i: (0, 0, i))
    return pl.pallas_call(
        _psel_kernel,
        grid=(rows // tm,),
        in_specs=[
            pl.BlockSpec((tm, d), lambda i: (i, 0)),
            pl.BlockSpec((d, nq), lambda i: (0, 0)),
            pl.BlockSpec((2 * P_HEADS, nk, LANES), lambda i: (0, 0, 0)),
        ],
        out_specs=[ospec, ospec, ospec, ospec],
        out_shape=[shp_f, shp_f, shp_f, shp_f],
        scratch_shapes=[pltpu.VMEM((2 * P_HEADS, tm, LANES), BF16),
                        pltpu.VMEM((2, P_TOPK, tm), F32),
                        pltpu.VMEM((CAND_ROWS, tm), F32),
                        pltpu.VMEM((P_TOPK, tm), F32),
                        pltpu.VMEM((CAND_ROWS, tm), F32)],
        compiler_params=_params("arbitrary"),
        name="psel",
    )(h2, w_pq, keys)


def _peer_kernel(h_ref, u_ref, vt_ref, rank2_ref, e2_ref, rc_ref, e1_ref, o_ref, w0_scr, w1_scr, *, te, tc, n_tiles):
    k = pl.program_id(1)
    nk = rank2_ref.shape[1]
    tm = h_ref.shape[0]
    nch = tm // tc

    def pre_chunk(c):
        return lax.dot_general(u_ref[...], h_ref[c * tc:(c + 1) * tc, :], NT_DIMS,
                               preferred_element_type=F32)

    def gate_chunk(c, w_ref, pre):
        for ab in range(te // nk):
            a = k * (te // nk) + ab
            rc_rows = [rc_ref[hh, pl.ds(a, 1), :] for hh in range(P_HEADS)]
            e1_rows = [e1_ref[hh, pl.ds(a, 1), :] for hh in range(P_HEADS)]
            for tg in range(tc // LANES):
                ls = slice(c * tc + tg * LANES, c * tc + (tg + 1) * LANES)
                g = jnp.zeros((nk, LANES), F32)
                for hh in range(P_HEADS):
                    g = g + jnp.where(rank2_ref[hh, :, ls] < rc_rows[hh][:, ls], e2_ref[hh, :, ls], 0.0) * e1_rows[hh][:, ls]
                x = pre[ab * nk:(ab + 1) * nk, tg * LANES:(tg + 1) * LANES]
                act = x * (1.0 + lax.erf(x * np.float32(math.sqrt(0.5))))
                w_ref[ab * nk:(ab + 1) * nk, ls] = (g * act).astype(BF16)

    def flush_chunk(c, w_ref, first):
        cols = slice(c * tc, (c + 1) * tc)
        upd = jnp.dot(vt_ref[...], w_ref[:, cols], preferred_element_type=F32)
        o_ref[:, cols] = upd if first else o_ref[:, cols] + upd

    def run(w_new, w_old, first_flush=False):
        for c in range(nch):
            if w_new is not None:
                gate_chunk(c, w_new, pre_chunk(c))
            if w_old is not None:
                flush_chunk(c, w_old, first_flush)

    w = (w0_scr, w1_scr)
    parity = k % 2

    @pl.when(k == 0)
    def _():
        run(w[0], None)

    @pl.when(k == 1)
    def _():
        run(w[1], w[0], first_flush=True)

    for par in range(2):
        @pl.when(jnp.logical_and(jnp.logical_and(k > 1, k < n_tiles), parity == par))
        def _(par=par):
            run(w[par], w[1 - par])

    @pl.when(k == n_tiles)
    def _():
        run(None, w[(n_tiles - 1) % 2])


PEER_TE = 512
PEER_TC = 256


def _peer(h2, u_bf, vt_bf, rank2, e2, rc, e1, tm, te, tc):
    rows, d = h2.shape
    ne = u_bf.shape[0]
    nk = rank2.shape[1]
    n_tiles = ne // te
    assert n_tiles >= 2
    sspec = pl.BlockSpec((P_HEADS, nk, tm), lambda i, k: (0, 0, i))
    return pl.pallas_call(
        functools.partial(_peer_kernel, te=te, tc=tc, n_tiles=n_tiles),
        grid=(rows // tm, n_tiles + 1),
        in_specs=[
            pl.BlockSpec((tm, d), lambda i, k: (i, 0)),
            pl.BlockSpec((te, d), lambda i, k: (jnp.minimum(k, n_tiles - 1), 0)),
            pl.BlockSpec((d, te), lambda i, k: (0, jnp.maximum(k - 1, 0))),
            sspec, sspec, sspec, sspec,
        ],
        out_specs=pl.BlockSpec((d, tm), lambda i, k: (0, i)),
        out_shape=jax.ShapeDtypeStruct((d, rows), F32),
        scratch_shapes=[pltpu.VMEM((te, tm), BF16), pltpu.VMEM((te, tm), BF16)],
        compiler_params=_params("arbitrary", "arbitrary"),
        name="peer",
    )(h2, u_bf, vt_bf, rank2, e2, rc, e1)


def _final_kernel(x1_ref, pt_ref, g2_ref, fw_ref, o_ref):
    x2 = x1_ref[...] + g2_ref[...] * pt_ref[...].T
    o_ref[...] = x2 * lax.rsqrt(jnp.mean(x2 * x2, axis=-1, keepdims=True) + EPS) * fw_ref[...]


def _final(x1, peer_t, mod3, final_w, blocks_per_batch, tm):
    rows, d = x1.shape
    return pl.pallas_call(
        _final_kernel,
        grid=(rows // tm,),
        in_specs=[
            pl.BlockSpec((tm, d), lambda i: (i, 0)),
            pl.BlockSpec((d, tm), lambda i: (0, i)),
            pl.BlockSpec((None, 1, d), lambda i: (i // blocks_per_batch, 0, 5)),
            pl.BlockSpec((1, d), lambda i: (0, 0)),
        ],
        out_specs=pl.BlockSpec((tm, d), lambda i: (i, 0)),
        out_shape=jax.ShapeDtypeStruct((rows, d), F32),
        compiler_params=_params("arbitrary"),
        name="final",
    )(x1, peer_t, mod3, final_w)


def _rope_tables(t):
    rows = t // GRID_W
    row = jnp.repeat(jnp.arange(rows, dtype=F32), GRID_W)
    col = jnp.tile(jnp.arange(GRID_W, dtype=F32), rows)
    axis_dim = A_HALF_DIM // 2
    inv_freq = ROPE_BASE ** (-jnp.arange(0, axis_dim, 2, dtype=F32) / axis_dim)
    ang = jnp.concatenate([row[:, None] * inv_freq, col[:, None] * inv_freq], axis=-1)
    cos = jnp.tile(jnp.cos(ang), (1, LANES // axis_dim))
    sin = jnp.tile(jnp.concatenate([-jnp.sin(ang), jnp.sin(ang)], axis=-1), (1, LANES // (2 * axis_dim)))
    return cos, sin


def kernel(x, c, ctx, c_ctx, w_mod, b_mod, norm1_w, w_in, b_in, conv_w, conv_b, m_norm_w, lambdas, a_norm_w,
           w_pa, w_pb, w_out, norm2_w, w_pq, sub_keys, expert_u, expert_v, final_norm_w):
    depth = w_mod.shape[0]
    assert depth == 1, "single-layer configuration only (context outputs never reach a latent token)"
    bsz, t, d = x.shape
    t_ctx = ctx.shape[1]
    mw = M_HEADS * M_HEAD_DIM
    aw = A_HEADS * 2 * A_HALF_DIM
    assert w_in.shape[2] == 4 * mw + 4 * M_HEADS + 3 * aw + 2 * d and d == 2 * TILE_N and mw == TILE_N == aw
    lam_init = 0.8 - 0.6 * math.exp(-0.3 * 0)

    mod_rows = -(-(bsz + 1) // SUBLANES) * SUBLANES
    cvec = jnp.zeros((mod_rows, d), F32).at[:bsz].set(c).at[bsz].set(c_ctx)
    mod3 = _mod(cvec, w_mod[0], b_mod[0]).reshape(mod_rows, 1, 6 * d)

    wi, bi = w_in[0], b_in[0]
    offs = np.cumsum((0, mw, mw, mw, mw, 4 * M_HEADS, aw, aw, aw, d, d))
    seg = {n: slice(int(offs[i]), int(offs[i + 1]))
           for i, n in enumerate(("qm", "km", "vm", "zm", "g", "qa", "ka", "va", "ga", "gb"))}
    order = ("ga", "gb", "qm", "km", "zm", "vm", "qa", "ka", "va")
    w_main = jnp.concatenate([wi[:, seg[n]] for n in order], axis=1).astype(BF16)
    b_main = jnp.concatenate([bi[seg[n]] for n in order]).reshape(1, -1)
    gpad = LANES - 2 * M_HEADS
    wg, bg = wi[:, seg["g"]], bi[seg["g"]]
    w_g = jnp.concatenate([wg[:, :2 * M_HEADS], jnp.zeros((d, gpad), F32),
                           wg[:, 2 * M_HEADS:], jnp.zeros((d, gpad), F32)], axis=1).astype(BF16)
    b_g = jnp.concatenate([bg[:2 * M_HEADS], jnp.zeros((gpad,), F32),
                           bg[2 * M_HEADS:], jnp.zeros((gpad,), F32)]).reshape(1, -1)

    cos, sin = _rope_tables(t)
    cos_l = jnp.tile(cos, (bsz, 1))
    sin_l = jnp.tile(sin, (bsz, 1))
    n1 = norm1_w[0].reshape(1, d)

    tm_in = 1024
    blocks_per_batch = t // tm_in
    of_lat, ob_lat, og_lat = _inproj(x.reshape(bsz * t, d), mod3, lambda i: i // blocks_per_batch, n1,
                                     w_main, b_main, w_g, b_g, cos_l, sin_l, tm_in)
    rows_c = bsz * t_ctx
    of_ctx, ob_ctx, og_ctx = _inproj(ctx.reshape(rows_c, d), mod3, lambda i: bsz, n1, w_main, b_main, w_g, b_g,
                                     jnp.ones((rows_c, LANES), F32), jnp.zeros((rows_c, LANES), F32), rows_c)

    qk_lat = _conv(of_lat.reshape(bsz, t, -1), conv_w[0], conv_b[0], 512)
    qk_ctx = _conv(of_ctx.reshape(bsz, t_ctx, -1), conv_w[0], conv_b[0], t_ctx)
    ob_lat3 = ob_lat.reshape(bsz, t, -1)
    ob_ctx3 = ob_ctx.reshape(bsz, t_ctx, -1)
    hdir = _mlstm(qk_ctx, ob_ctx3, og_ctx.reshape(bsz, t_ctx, -1), qk_lat, ob_lat3, og_lat.reshape(bsz, t, -1))

    lq1, lk1, lq2, lk2 = lambdas[0].astype(F32)
    lam = (jnp.exp(jnp.sum(lq1 * lk1)) - jnp.exp(jnp.sum(lq2 * lk2)) + lam_init).reshape(1, 1)
    o_attn = _attn(lam, ob_lat3, ob_ctx3, ATTN_TQ, min(ATTN_TK, t), ATTN_RC)

    merged = _merge(hdir.reshape(2, bsz * t, mw), of_lat, o_attn.reshape(bsz * t, aw),
                    m_norm_w[0].reshape(1, mw), jnp.tile(a_norm_w[0], A_HEADS).reshape(1, aw),
                    w_pa[0].astype(BF16), w_pb[0].astype(BF16), 1.0 - lam_init, 256)
    tm_r = 256
    x1, h2 = _resid(merged, w_out[0].astype(BF16), x.reshape(bsz * t, d), mod3, norm2_w[0].reshape(1, d),
                    t // tm_r, tm_r)

    nk = sub_keys.shape[3]
    keys = sub_keys[0].reshape(2 * P_HEADS, nk, sub_keys.shape[4]).astype(BF16)
    rank2, e2, rc, e1 = _psel(h2, w_pq[0].astype(BF16), keys, 256)
    tm_p = 512
    peer_t = _peer(h2, expert_u[0].astype(BF16), expert_v[0].T.astype(BF16), rank2, e2, rc, e1, tm_p, PEER_TE, PEER_TC)

    tm_f = 256
    out = _final(x1, peer_t, mod3, final_norm_w.reshape(1, d), t // tm_f, tm_f)
    return out.reshape(bsz, t, d)
```

```python
import functools
import math

import jax
import jax.numpy as jnp
import numpy as np
from jax import lax
from jax.experimental import pallas as pl
from jax.experimental.pallas import tpu as pltpu

F32 = jnp.float32
BF16 = jnp.bfloat16

EPS = 1e-6
GRID_W = 64
ROPE_BASE = 10000.0
M_HEADS = 8
M_HEAD_DIM = 128
M_CHUNK = 128
CONV_K = 5
A_HEADS = 8
A_HALF_DIM = 64
P_HEADS = 8
P_TOPK = 16
LANES = 128
SUBLANES = 8
VMEM_LIMIT = 56 * 1024 * 1024

NT_DIMS = (((1,), (1,)), ((), ()))
TN_DIMS = (((0,), (0,)), ((), ()))


def _params(*sem):
    return pltpu.CompilerParams(dimension_semantics=sem, vmem_limit_bytes=VMEM_LIMIT)


def _sigmoid(x):
    return 1.0 / (1.0 + jnp.exp(-x))


def _mod_kernel(c_ref, w_ref, b_ref, o_ref):
    c = c_ref[...]
    s = (c * _sigmoid(c)).astype(BF16)
    o_ref[...] = jnp.dot(s, w_ref[...].astype(BF16), preferred_element_type=F32) + b_ref[...]


def _mod(cvec, w_mod, b_mod):
    rows, d = cvec.shape
    n = w_mod.shape[1]
    tn = 1024
    return pl.pallas_call(
        _mod_kernel,
        grid=(n // tn,),
        in_specs=[pl.BlockSpec((rows, d), lambda j: (0, 0)),
                  pl.BlockSpec((d, tn), lambda j: (0, j)),
                  pl.BlockSpec((1, tn), lambda j: (0, j))],
        out_specs=pl.BlockSpec((rows, tn), lambda j: (0, j)),
        out_shape=jax.ShapeDtypeStruct((rows, n), F32),
        compiler_params=_params("arbitrary"),
        name="mod",
    )(cvec, w_mod, b_mod.reshape(1, n))


N_F32_TILES = 7
N_BF16_TILES = 4
TILE_N = 1024


def _inproj_kernel(x_ref, n1_ref, sh_ref, sc_ref, w_ref, b_ref, wg_ref, bg_ref, cos_ref, sin_ref,
                   of_ref, ob_ref, og_ref, h_ref):
    j = pl.program_id(1)

    @pl.when(j == 0)
    def _():
        x = x_ref[...]
        y = x * lax.rsqrt(jnp.mean(x * x, axis=-1, keepdims=True) + EPS) * n1_ref[...]
        h = (y * (1.0 + sc_ref[...]) + sh_ref[...]).astype(BF16)
        h_ref[...] = h
        og_ref[...] = jnp.dot(h, wg_ref[...], preferred_element_type=F32) + bg_ref[...]

    acc = jnp.dot(h_ref[...], w_ref[...], preferred_element_type=F32) + b_ref[...]

    @pl.when(j < N_F32_TILES)
    def _():
        of_ref[...] = acc

    @pl.when(jnp.logical_or(j == N_F32_TILES, j == N_F32_TILES + 3))
    def _():
        ob_ref[...] = acc.astype(BF16)

    def rope(scale):
        cos = cos_ref[...]
        sin = sin_ref[...]
        lane = lax.broadcasted_iota(jnp.int32, cos.shape, 1)
        first = (lane % (2 * (A_HALF_DIM // 2))) < (A_HALF_DIM // 2)
        for hh in range(TILE_N // LANES):
            a = acc[:, hh * LANES:(hh + 1) * LANES]
            r = jnp.where(first, pltpu.roll(a, LANES - A_HALF_DIM // 2, 1), pltpu.roll(a, A_HALF_DIM // 2, 1))
            ob_ref[:, hh * LANES:(hh + 1) * LANES] = ((a * cos + r * sin) * scale).astype(BF16)

    @pl.when(j == N_F32_TILES + 1)
    def _():
        rope(A_HALF_DIM ** -0.5)

    @pl.when(j == N_F32_TILES + 2)
    def _():
        rope(1.0)


def _inproj(x2, mod3, seg_of_block, norm_w, w_main, b_main, w_g, b_g, cos, sin, tm):
    rows, d = x2.shape
    nj = N_F32_TILES + N_BF16_TILES
    grid = (rows // tm, nj)
    return pl.pallas_call(
        _inproj_kernel,
        grid=grid,
        in_specs=[
            pl.BlockSpec((tm, d), lambda i, j: (i, 0)),
            pl.BlockSpec((1, d), lambda i, j: (0, 0)),
            pl.BlockSpec((None, 1, d), lambda i, j: (seg_of_block(i), 0, 0)),
            pl.BlockSpec((None, 1, d), lambda i, j: (seg_of_block(i), 0, 1)),
            pl.BlockSpec((d, TILE_N), lambda i, j: (0, j)),
            pl.BlockSpec((1, TILE_N), lambda i, j: (0, j)),
            pl.BlockSpec((d, 2 * LANES), lambda i, j: (0, 0)),
            pl.BlockSpec((1, 2 * LANES), lambda i, j: (0, 0)),
            pl.BlockSpec((tm, LANES), lambda i, j: (i, 0)),
            pl.BlockSpec((tm, LANES), lambda i, j: (i, 0)),
        ],
        out_specs=[
            pl.BlockSpec((tm, TILE_N), lambda i, j: (i, jnp.minimum(j, N_F32_TILES - 1))),
            pl.BlockSpec((tm, TILE_N), lambda i, j: (i, jnp.maximum(j - N_F32_TILES, 0))),
            pl.BlockSpec((tm, 2 * LANES), lambda i, j: (i, 0)),
        ],
        out_shape=[
            jax.ShapeDtypeStruct((rows, N_F32_TILES * TILE_N), F32),
            jax.ShapeDtypeStruct((rows, N_BF16_TILES * TILE_N), BF16),
            jax.ShapeDtypeStruct((rows, 2 * LANES), F32),
        ],
        scratch_shapes=[pltpu.VMEM((tm, d), BF16)],
        compiler_params=_params("arbitrary", "arbitrary"),
        name="inproj",
    )(x2, norm_w, mod3, mod3, w_main, b_main, w_g, b_g, cos, sin)


QM_TILE = 4


def _conv_kernel(prev_ref, cur_ref, next_ref, w_ref, b_ref, o_ref, buf_ref, *, q_scale):
    i = pl.program_id(1)
    n = pl.num_programs(1)
    jc = pl.program_id(2)
    tc = cur_ref.shape[0]
    halo = SUBLANES
    buf_ref[0:halo, :] = jnp.where(i > 0, prev_ref[...], 0.0)
    buf_ref[halo:halo + tc, :] = cur_ref[...]
    buf_ref[halo + tc:2 * halo + tc, :] = jnp.where(i < n - 1, next_ref[...], 0.0)
    acc = jnp.zeros(cur_ref.shape, F32) + b_ref[...]
    for k in range(CONV_K):
        acc = acc + buf_ref[pl.ds(halo + k - CONV_K // 2, tc), :] * w_ref[k:k + 1, :]
    y = acc * _sigmoid(acc)
    y = y * jnp.where(jc == 0, q_scale, 1.0)
    o_ref[...] = y.astype(BF16)


def _conv(of3, conv_w, conv_b, tc):
    b, t, ncol = of3.shape
    w = M_HEADS * M_HEAD_DIM
    of4 = of3.reshape(b, t // SUBLANES, SUBLANES, ncol)
    g = tc // SUBLANES
    last = t // SUBLANES - 1
    return pl.pallas_call(
        functools.partial(_conv_kernel, q_scale=M_HEAD_DIM ** -0.5),
        grid=(b, t // tc, 2),
        in_specs=[
            pl.BlockSpec((None, None, SUBLANES, w), lambda bb, i, jc: (bb, jnp.maximum(i * g - 1, 0), 0, QM_TILE + jc)),
            pl.BlockSpec((None, tc, w), lambda bb, i, jc: (bb, i, QM_TILE + jc)),
            pl.BlockSpec((None, None, SUBLANES, w), lambda bb, i, jc: (bb, jnp.minimum((i + 1) * g, last), 0, QM_TILE + jc)),
            pl.BlockSpec((CONV_K, w), lambda bb, i, jc: (0, jc)),
            pl.BlockSpec((1, w), lambda bb, i, jc: (0, jc)),
        ],
        out_specs=pl.BlockSpec((None, tc, w), lambda bb, i, jc: (bb, i, jc)),
        out_shape=jax.ShapeDtypeStruct((b, t, 2 * w), BF16),
        scratch_shapes=[pltpu.VMEM((tc + 2 * SUBLANES, w), F32)],
        compiler_params=_params("arbitrary", "arbitrary", "arbitrary"),
        name="conv",
    )(of4, of3, of4, conv_w, conv_b.reshape(1, 2 * w))


def _split3(x):
    hi = x.astype(BF16)
    r = x - hi.astype(F32)
    mid = r.astype(BF16)
    lo = (r - mid.astype(F32)).astype(BF16)
    return hi, mid, lo


def _log_sigmoid(x):
    return jnp.minimum(x, 0.0) - jnp.log1p(jnp.exp(-jnp.abs(x)))


P_LANE = 0
B_LANE = M_HEADS
LW_LANE = 2 * M_HEADS


def _split2(x):
    hi = x.astype(BF16)
    return hi, (x - hi.astype(F32)).astype(BF16)


def _running_max(y, reverse):
    n = y.shape[0]
    row = lax.broadcasted_iota(jnp.int32, y.shape, 0)
    k = 1
    while k < n:
        fwd = jnp.where(row >= k, pltpu.roll(y, k, 0), -jnp.inf)
        bwd = jnp.where(row < n - k, pltpu.roll(y, n - k, 0), -jnp.inf)
        y = jnp.maximum(y, jnp.where(reverse, bwd, fwd))
        k *= 2
    return y


def _mlstm_kernel(sel_ref, qc_ref, kc_ref, vc_ref, gc_ref, ql_ref, kl_ref, vl_ref, gl_ref, o_ref,
                  c_ref, m_ref, *, n_ctx_chunks):
    d = pl.program_id(1)
    c = pl.program_id(2)
    L = M_CHUNK
    dh = M_HEAD_DIM

    @pl.when(c == 0)
    def _():
        c_ref[...] = jnp.zeros(c_ref.shape, F32)
        m_ref[...] = jnp.zeros(m_ref.shape, F32)

    def twice(x):
        return jnp.concatenate([x, x], axis=1)

    def step(q_ref, k_ref, v_ref, g_ref, with_output):
        row = lax.broadcasted_iota(jnp.int32, (L, L), 0)
        col = lax.broadcasted_iota(jnp.int32, (L, L), 1)
        causal = (row - col) * (1 - 2 * d) >= 0
        tri = causal.astype(BF16)
        g = g_ref[...]
        lf = _log_sigmoid(g)
        hi, mid, lo = _split3(lf)
        b = (jnp.dot(tri, hi, preferred_element_type=F32) + jnp.dot(tri, mid, preferred_element_type=F32)
             + jnp.dot(tri, lo, preferred_element_type=F32))
        btot = jnp.sum(lf, axis=0, keepdims=True)
        y = g - pltpu.roll(b, LANES - M_HEADS, 1)
        pmax = _running_max(y, d == 1)
        lw = pltpu.roll(btot - b, LANES - M_HEADS, 1) + g
        lane = lax.broadcasted_iota(jnp.int32, (L, LANES), 1)
        table = jnp.where(lane < B_LANE, pmax, jnp.where(lane < LW_LANE, b, pltpu.roll(lw, LW_LANE, 1)))
        t_hi, t_lo = _split2(table)
        y_t = y.T
        ones = jnp.ones((L, dh), BF16)
        for h in range(M_HEADS):
            hs = slice(h * dh, (h + 1) * dh)
            q = q_ref[:, hs]
            k = k_ref[:, hs]
            v_ext = jnp.concatenate([v_ref[:, hs], ones], axis=1)
            sel = sel_ref[h]
            rep = jnp.dot(t_hi, sel, preferred_element_type=F32) + jnp.dot(t_lo, sel, preferred_element_type=F32)
            p_c = rep[:, 0:L]
            b_c = rep[:, L:2 * L]
            lw_c = rep[:, 2 * L:3 * L]
            m_prev = m_ref[h]
            c_ext = c_ref[h]
            if with_output:
                mx = jnp.maximum(m_prev, p_c)
                inter = jnp.exp(m_prev - mx)
                dmat = jnp.where(causal, jnp.exp(y_t[h:h + 1, :] - mx), 0.0)
                s = lax.dot_general(q, k, NT_DIMS, preferred_element_type=F32)
                qk_f = s * dmat
                qk = qk_f.astype(BF16)
                qk_lo = (qk_f - qk.astype(F32)).astype(BF16)
                c_hi = c_ext.astype(BF16)
                n_lo = (c_ext[:, dh:2 * dh] - c_hi[:, dh:2 * dh].astype(F32)).astype(BF16)
                acc = (twice(inter) * jnp.dot(q, c_hi, preferred_element_type=F32)
                       + jnp.dot(qk, v_ext, preferred_element_type=F32))
                den = (acc[:, dh:2 * dh] + inter * jnp.dot(q, n_lo, preferred_element_type=F32)
                       + jnp.dot(qk_lo, ones, preferred_element_type=F32))
                o_ref[:, hs] = acc[:, 0:dh] / jnp.maximum(jnp.abs(den), jnp.exp(-(b_c + mx)))
            b_end = jnp.min(b_c, axis=0, keepdims=True)
            m_end = jnp.maximum(b_end + m_prev, jnp.max(lw_c, axis=0, keepdims=True))
            decay = jnp.exp(b_end + m_prev - m_end)
            kw_f = k.astype(F32) * jnp.exp(lw_c - m_end)
            kw = kw_f.astype(BF16)
            kw_lo = (kw_f - kw.astype(F32)).astype(BF16)
            upd = lax.dot_general(kw, v_ext, TN_DIMS, preferred_element_type=F32)
            n_fix = lax.dot_general(kw_lo, ones, TN_DIMS, preferred_element_type=F32)
            upd = upd + jnp.concatenate([jnp.zeros_like(n_fix), n_fix], axis=1)
            c_ref[h] = twice(decay) * c_ext + upd
            m_ref[h] = m_end

    @pl.when(c < n_ctx_chunks)
    def _():
        step(qc_ref, kc_ref, vc_ref, gc_ref, False)

    @pl.when(c >= n_ctx_chunks)
    def _():
        step(ql_ref, kl_ref, vl_ref, gl_ref, True)


def _mlstm_select_table():
    sel = np.zeros((M_HEADS, LANES, 3 * M_CHUNK), np.float32)
    for h in range(M_HEADS):
        for j, base in enumerate((P_LANE, B_LANE, LW_LANE)):
            sel[h, base + h, j * M_CHUNK:(j + 1) * M_CHUNK] = 1.0
    return jnp.asarray(sel, BF16)


def _mlstm(qk_ctx, ob_ctx, og_ctx, qk_lat, ob_lat, og_lat):
    b, tc_, _ = qk_ctx.shape
    t = qk_lat.shape[1]
    w = M_HEADS * M_HEAD_DIM
    ncc = tc_ // M_CHUNK
    ncl = t // M_CHUNK

    def ctx_idx(d, c):
        cc = jnp.minimum(c, ncc - 1)
        return jnp.where(d == 0, cc, ncc - 1 - cc)

    def lat_idx(d, c):
        cl = jnp.maximum(c - ncc, 0)
        return jnp.where(d == 0, cl, ncl - 1 - cl)

    def cspec(width, colfn):
        return pl.BlockSpec((None, M_CHUNK, width), lambda bb, d, c: (bb, ctx_idx(d, c), colfn(d)))

    def lspec(width, colfn):
        return pl.BlockSpec((None, M_CHUNK, width), lambda bb, d, c: (bb, lat_idx(d, c), colfn(d)))

    return pl.pallas_call(
        functools.partial(_mlstm_kernel, n_ctx_chunks=ncc),
        grid=(b, 2, ncc + ncl),
        in_specs=[pl.BlockSpec((M_HEADS, LANES, 3 * M_CHUNK), lambda bb, d, c: (0, 0, 0)),
                  cspec(w, lambda d: 0), cspec(w, lambda d: 1), cspec(w, lambda d: 0), cspec(LANES, lambda d: d),
                  lspec(w, lambda d: 0), lspec(w, lambda d: 1), lspec(w, lambda d: 0), lspec(LANES, lambda d: d)],
        out_specs=pl.BlockSpec((None, None, M_CHUNK, w), lambda bb, d, c: (d, bb, lat_idx(d, c), 0)),
        out_shape=jax.ShapeDtypeStruct((2, b, t, w), F32),
        scratch_shapes=[pltpu.VMEM((M_HEADS, M_HEAD_DIM, 2 * M_HEAD_DIM), F32),
                        pltpu.VMEM((M_HEADS, 1, M_CHUNK), F32)],
        compiler_params=_params("arbitrary", "arbitrary", "arbitrary"),
        name="mlstm",
    )(_mlstm_select_table(), qk_ctx, qk_ctx, ob_ctx, og_ctx, qk_lat, qk_lat, ob_lat, og_lat)


Q_TILE_COL = 8
K_TILE_COL = 16
V_TILE_COL = 24


def _attn_kernel(lam_ref, q_ref, k_ref, v_in_ref, kc_ref, vc_in_ref, o_ref, qs_ref, m_ref, acc_ref, v_ref, vc_ref,
                 *, tk, rc):
    tq = q_ref.shape[0]

    @pl.when(pl.program_id(2) == 0)
    def _():
        for src, dst in ((v_in_ref, v_ref), (vc_in_ref, vc_ref)):
            lane = lax.broadcasted_iota(jnp.int32, src.shape, 1)
            dst[:, 0:LANES] = src[...]
            dst[:, LANES:2 * LANES] = jnp.where(lane == 0, 1.0, 0.0).astype(BF16)

    q = q_ref[...]
    lane = lax.broadcasted_iota(jnp.int32, q.shape, 1)
    zero = jnp.zeros_like(q)
    qs_ref[0:tq, :] = jnp.where(lane < A_HALF_DIM, q, zero)
    qs_ref[tq:2 * tq, :] = jnp.where(lane >= A_HALF_DIM, q, zero)

    def block(kb, vb, first):
        for c in range(2 * tq // rc):
            rs = slice(c * rc, (c + 1) * rc)
            s = lax.dot_general(qs_ref[rs, :], kb, NT_DIMS, preferred_element_type=F32)
            smax = jnp.max(s, axis=1, keepdims=True)
            if first:
                m_new = smax
                p = jnp.exp(s - m_new)
                acc_ref[rs, :] = jnp.dot(p.astype(BF16), vb, preferred_element_type=F32)
            else:
                m_prev = m_ref[rs, :]
                m_new = jnp.maximum(m_prev, smax)
                alpha = jnp.exp(m_prev - m_new)
                p = jnp.exp(s - m_new)
                acc_ref[rs, :] = alpha * acc_ref[rs, :] + jnp.dot(p.astype(BF16), vb, preferred_element_type=F32)
            m_ref[rs, :] = m_new

    block(kc_ref[...], vc_ref[...], True)

    def body(i, carry):
        off = pl.multiple_of(i * tk, tk)
        block(k_ref[pl.ds(off, tk), :], v_ref[pl.ds(off, tk), :], False)
        return carry

    lax.fori_loop(0, k_ref.shape[0] // tk, body, 0)
    o = acc_ref[:, 0:LANES] / acc_ref[:, LANES:LANES + 1]
    o_ref[...] = o[0:tq] - lam_ref[0, 0] * o[tq:2 * tq]


ATTN_TQ = 1024
ATTN_TK = 4096
ATTN_RC = 128


def _attn(lam, ob_lat, ob_ctx, tq, tk, rc):
    b, t, _ = ob_lat.shape
    tc_ = ob_ctx.shape[1]
    w = A_HEADS * 2 * A_HALF_DIM
    return pl.pallas_call(
        functools.partial(_attn_kernel, tk=tk, rc=rc),
        grid=(b, A_HEADS, t // tq),
        in_specs=[
            pl.BlockSpec(memory_space=pltpu.SMEM),
            pl.BlockSpec((None, tq, LANES), lambda bb, h, i: (bb, i, Q_TILE_COL + h)),
            pl.BlockSpec((None, t, LANES), lambda bb, h, i: (bb, 0, K_TILE_COL + h)),
            pl.BlockSpec((None, t, LANES), lambda bb, h, i: (bb, 0, V_TILE_COL + h)),
            pl.BlockSpec((None, tc_, LANES), lambda bb, h, i: (bb, 0, K_TILE_COL + h)),
            pl.BlockSpec((None, tc_, LANES), lambda bb, h, i: (bb, 0, V_TILE_COL + h)),
        ],
        out_specs=pl.BlockSpec((None, tq, LANES), lambda bb, h, i: (bb, i, h)),
        out_shape=jax.ShapeDtypeStruct((b, t, w), F32),
        scratch_shapes=[pltpu.VMEM((2 * tq, LANES), BF16),
                        pltpu.VMEM((2 * tq, 1), F32),
                        pltpu.VMEM((2 * tq, 2 * LANES), F32),
                        pltpu.VMEM((t, 2 * LANES), BF16),
                        pltpu.VMEM((tc_, 2 * LANES), BF16)],
        compiler_params=_params("arbitrary", "arbitrary", "arbitrary"),
        name="attn",
    )(lam, ob_lat, ob_lat, ob_lat, ob_ctx, ob_ctx)


GA_TILE = 0
GB_TILE = 1
ZM_TILE = 6


def _head_rmsnorm(x, w, n_heads, dh):
    outs = []
    for h in range(n_heads):
        blk = x[:, h * dh:(h + 1) * dh]
        outs.append(blk * lax.rsqrt(jnp.mean(blk * blk, axis=-1, keepdims=True) + EPS))
    return jnp.concatenate(outs, axis=1) * w


def _merge_kernel(hf_ref, hb_ref, z_ref, o_ref, ga_ref, gb_ref, mnw_ref, anw_ref, wpa_ref, wpb_ref, out_ref,
                  *, d_scale):
    hsum = hf_ref[...] + hb_ref[...]
    ym = _sigmoid(z_ref[...]) * _head_rmsnorm(hsum, mnw_ref[...], M_HEADS, M_HEAD_DIM)
    yd = _head_rmsnorm(o_ref[...], anw_ref[...], A_HEADS, 2 * A_HALF_DIM) * d_scale
    pa = jnp.dot(ym.astype(BF16), wpa_ref[...], preferred_element_type=F32)
    pb = jnp.dot(yd.astype(BF16), wpb_ref[...], preferred_element_type=F32)
    out_ref[...] = (_sigmoid(ga_ref[...]) * pa + _sigmoid(gb_ref[...]) * pb).astype(BF16)


def _merge(hdir, of_lat, o_attn, mnw, anw, w_pa, w_pb, d_scale, tm):
    rows = of_lat.shape[0]
    w = M_HEADS * M_HEAD_DIM
    d = w_pa.shape[1]
    return pl.pallas_call(
        functools.partial(_merge_kernel, d_scale=d_scale),
        grid=(rows // tm,),
        in_specs=[
            pl.BlockSpec((None, tm, w), lambda i: (0, i, 0)),
            pl.BlockSpec((None, tm, w), lambda i: (1, i, 0)),
            pl.BlockSpec((tm, w), lambda i: (i, ZM_TILE)),
            pl.BlockSpec((tm, w), lambda i: (i, 0)),
            pl.BlockSpec((tm, d), lambda i: (i, GA_TILE)),
            pl.BlockSpec((tm, d), lambda i: (i, GB_TILE)),
            pl.BlockSpec((1, w), lambda i: (0, 0)),
            pl.BlockSpec((1, w), lambda i: (0, 0)),
            pl.BlockSpec((w, d), lambda i: (0, 0)),
            pl.BlockSpec((w, d), lambda i: (0, 0)),
        ],
        out_specs=pl.BlockSpec((tm, d), lambda i: (i, 0)),
        out_shape=jax.ShapeDtypeStruct((rows, d), BF16),
        compiler_params=_params("arbitrary"),
        name="merge",
    )(hdir, hdir, of_lat, o_attn, of_lat, of_lat, mnw, anw, w_pa, w_pb)


def _resid_kernel(mg_ref, wout_ref, x_ref, g1_ref, n2_ref, sh_ref, sc_ref, x1_ref, h2_ref):
    y = jnp.dot(mg_ref[...], wout_ref[...], preferred_element_type=F32)
    x1 = x_ref[...] + g1_ref[...] * y
    x1_ref[...] = x1
    hn = x1 * lax.rsqrt(jnp.mean(x1 * x1, axis=-1, keepdims=True) + EPS) * n2_ref[...]
    h2_ref[...] = (hn * (1.0 + sc_ref[...]) + sh_ref[...]).astype(BF16)


def _resid(merged, w_out, x2, mod3, norm2_w, blocks_per_batch, tm):
    rows, d = x2.shape

    def mspec(col):
        return pl.BlockSpec((None, 1, d), lambda i: (i // blocks_per_batch, 0, col))

    return pl.pallas_call(
        _resid_kernel,
        grid=(rows // tm,),
        in_specs=[
            pl.BlockSpec((tm, d), lambda i: (i, 0)),
            pl.BlockSpec((d, d), lambda i: (0, 0)),
            pl.BlockSpec((tm, d), lambda i: (i, 0)),
            mspec(2),
            pl.BlockSpec((1, d), lambda i: (0, 0)),
            mspec(3),
            mspec(4),
        ],
        out_specs=[pl.BlockSpec((tm, d), lambda i: (i, 0)), pl.BlockSpec((tm, d), lambda i: (i, 0))],
        out_shape=[jax.ShapeDtypeStruct((rows, d), F32), jax.ShapeDtypeStruct((rows, d), BF16)],
        compiler_params=_params("arbitrary"),
        name="resid",
    )(merged, w_out, x2, mod3, norm2_w, mod3, mod3)


CAND_COUNTS = tuple(P_TOPK // r for r in range(1, P_TOPK + 1))
N_CAND = sum(CAND_COUNTS)
CAND_ROWS = -(-N_CAND // SUBLANES) * SUBLANES
UNRANKED = 255.0


def _extract_top(s, count, store):
    rows = s.shape[0]
    ridx = lax.broadcasted_iota(jnp.int32, s.shape, 0)
    rank = jnp.full(s.shape, UNRANKED, F32)
    for r in range(count):
        m = jnp.max(s, axis=0, keepdims=True)
        first = jnp.min(jnp.where(s == m, ridx, rows), axis=0, keepdims=True)
        hit = ridx == first
        s = jnp.where(hit, -jnp.inf, s)
        rank = jnp.where(hit, float(r), rank)
        store(r, m)
    return rank


def _psel_kernel(h_ref, wpq_ref, keys_ref, rank2_ref, e2_ref, rc_ref, e1_ref, q_scr, top_scr, cand_scr, ctop_scr,
                 csel_scr):
    q = jnp.dot(h_ref[...], wpq_ref[...], preferred_element_type=F32).astype(BF16)
    for hp in range(2 * P_HEADS):
        q_scr[hp] = q[:, hp * LANES:(hp + 1) * LANES]
    cand_scr[...] = jnp.full(cand_scr.shape, -jnp.inf, F32)

    def head(hh, carry):
        s_half, rank_half = [], []
        for p in range(2):
            s = lax.dot_general(keys_ref[2 * hh + p], q_scr[2 * hh + p], NT_DIMS, preferred_element_type=F32)
            s_half.append(s)

            def store_top(r, row, p=p):
                top_scr[p, pl.ds(r, 1), :] = row

            rank_half.append(_extract_top(s, P_TOPK, store_top))
        off = 0
        for r1, cnt in enumerate(CAND_COUNTS):
            cand_scr[pl.ds(off, cnt), :] = top_scr[0, pl.ds(r1, 1), :] + top_scr[1, pl.ds(0, cnt), :]
            off += cnt

        def store_ctop(r, row):
            ctop_scr[pl.ds(r, 1), :] = row

        crank = _extract_top(cand_scr[...], P_TOPK, store_ctop)
        csel_scr[...] = jnp.where(crank < P_TOPK, 1.0, 0.0)
        ctop = ctop_scr[...]
        z = jnp.sum(jnp.exp(ctop - ctop[0:1, :]), axis=0, keepdims=True)
        (s1, s2), (rank1, rank2) = s_half, rank_half
        rc = jnp.zeros(s1.shape, F32)
        off = 0
        for r1, cnt in enumerate(CAND_COUNTS):
            picked = jnp.sum(csel_scr[pl.ds(off, cnt), :], axis=0, keepdims=True)
            rc = jnp.where(rank1 == float(r1), picked, rc)
            off += cnt
        rank2_ref[hh] = rank2
        e2_ref[hh] = jnp.exp(s2 - top_scr[1, pl.ds(0, 1), :])
        rc_ref[hh] = rc
        e1_ref[hh] = 0.5 * jnp.exp(s1 - top_scr[0, pl.ds(0, 1), :]) / z
        return carry

    lax.fori_loop(0, P_HEADS, head, 0)


def _psel(h2, w_pq, keys, tm):
    rows, d = h2.shape
    nk = keys.shape[1]
    nq = w_pq.shape[1]
    shp_f = jax.ShapeDtypeStruct((P_HEADS, nk, rows), F32)
    ospec = pl.BlockSpec((P_HEADS, nk, tm), lambda i: (0, 0, i))
    return pl.pallas_call(
        _psel_kernel,
        grid=(rows // tm,),
        in_specs=[
            pl.BlockSpec((tm, d), lambda i: (i, 0)),
            pl.BlockSpec((d, nq), lambda i: (0, 0)),
            pl.BlockSpec((2 * P_HEADS, nk, LANES), lambda i: (0, 0, 0)),
        ],
        out_specs=[ospec, ospec, ospec, ospec],
        out_shape=[shp_f, shp_f, shp_f, shp_f],
        scratch_shapes=[pltpu.VMEM((2 * P_HEADS, tm, LANES), BF16),
                        pltpu.VMEM((2, P_TOPK, tm), F32),
                        pltpu.VMEM((CAND_ROWS, tm), F32),
                        pltpu.VMEM((P_TOPK, tm), F32),
                        pltpu.VMEM((CAND_ROWS, tm), F32)],
        compiler_params=_params("arbitrary"),
        name="psel",
    )(h2, w_pq, keys)


def _peer_kernel(h_ref, u_ref, vt_ref, rank2_ref, e2_ref, rc_ref, e1_ref, o_ref, w0_scr, w1_scr, pre_scr,
                 *, te, tc, n_tiles):
    k = pl.program_id(1)
    nk = rank2_ref.shape[1]
    tm = h_ref.shape[0]
    nch = tm // tc

    def pre_chunk(c):
        pre_scr[:, c * tc:(c + 1) * tc] = lax.dot_general(u_ref[...], h_ref[c * tc:(c + 1) * tc, :], NT_DIMS,
                                                          preferred_element_type=F32)

    def gate_block(c, ab, w_ref):
        a = k * (te // nk) + ab
        rc_rows = [rc_ref[hh, pl.ds(a, 1), :] for hh in range(P_HEADS)]
        e1_rows = [e1_ref[hh, pl.ds(a, 1), :] for hh in range(P_HEADS)]
        for tg in range(tc // LANES):
            ls = slice(c * tc + tg * LANES, c * tc + (tg + 1) * LANES)
            g = jnp.zeros((nk, LANES), F32)
            for hh in range(P_HEADS):
                g = g + jnp.where(rank2_ref[hh, :, ls] < rc_rows[hh][:, ls], e2_ref[hh, :, ls], 0.0) * e1_rows[hh][:, ls]
            x = pre_scr[ab * nk:(ab + 1) * nk, ls]
            act = x * (1.0 + lax.erf(x * np.float32(math.sqrt(0.5))))
            w_ref[ab * nk:(ab + 1) * nk, ls] = (g * act).astype(BF16)

    n_ab = te // nk
    d = vt_ref.shape[0]

    def flush_piece(c, j, w_ref, first):
        cols = slice(c * tc, (c + 1) * tc)
        rows = slice(j * (d // n_ab), (j + 1) * (d // n_ab))
        upd = jnp.dot(vt_ref[rows, :], w_ref[:, cols], preferred_element_type=F32)
        o_ref[rows, cols] = upd if first else o_ref[rows, cols] + upd

    def run(w_new, w_old, first_flush=False):
        for c in range(nch):
            if w_new is not None:
                pre_chunk(c)
            for ab in range(n_ab):
                if w_new is not None:
                    gate_block(c, ab, w_new)
                if w_old is not None:
                    flush_piece(c, ab, w_old, first_flush)

    w = (w0_scr, w1_scr)
    parity = k % 2

    @pl.when(k == 0)
    def _():
        run(w[0], None)

    @pl.when(k == 1)
    def _():
        run(w[1], w[0], first_flush=True)

    for par in range(2):
        @pl.when(jnp.logical_and(jnp.logical_and(k > 1, k < n_tiles), parity == par))
        def _(par=par):
            run(w[par], w[1 - par])

    @pl.when(k == n_tiles)
    def _():
        run(None, w[(n_tiles - 1) % 2])


PEER_TE = 512
PEER_TC = 256


def _peer(h2, u_bf, vt_bf, rank2, e2, rc, e1, tm, te, tc):
    rows, d = h2.shape
    ne = u_bf.shape[0]
    nk = rank2.shape[1]
    n_tiles = ne // te
    assert n_tiles >= 2
    sspec = pl.BlockSpec((P_HEADS, nk, tm), lambda i, k: (0, 0, i))
    return pl.pallas_call(
        functools.partial(_peer_kernel, te=te, tc=tc, n_tiles=n_tiles),
        grid=(rows // tm, n_tiles + 1),
        in_specs=[
            pl.BlockSpec((tm, d), lambda i, k: (i, 0)),
            pl.BlockSpec((te, d), lambda i, k: (jnp.minimum(k, n_tiles - 1), 0)),
            pl.BlockSpec((d, te), lambda i, k: (0, jnp.maximum(k - 1, 0))),
            sspec, sspec, sspec, sspec,
        ],
        out_specs=pl.BlockSpec((d, tm), lambda i, k: (0, i)),
        out_shape=jax.ShapeDtypeStruct((d, rows), F32),
        scratch_shapes=[pltpu.VMEM((te, tm), BF16), pltpu.VMEM((te, tm), BF16), pltpu.VMEM((te, tm), F32)],
        compiler_params=_params("arbitrary", "arbitrary"),
        name="peer",
    )(h2, u_bf, vt_bf, rank2, e2, rc, e1)


def _final_kernel(x1_ref, pt_ref, g2_ref, fw_ref, o_ref):
    x2 = x1_ref[...] + g2_ref[...] * pt_ref[...].T
    o_ref[...] = x2 * lax.rsqrt(jnp.mean(x2 * x2, axis=-1, keepdims=True) + EPS) * fw_ref[...]


def _final(x1, peer_t, mod3, final_w, blocks_per_batch, tm):
    rows, d = x1.shape
    return pl.pallas_call(
        _final_kernel,
        grid=(rows // tm,),
        in_specs=[
            pl.BlockSpec((tm, d), lambda i: (i, 0)),
            pl.BlockSpec((d, tm), lambda i: (0, i)),
            pl.BlockSpec((None, 1, d), lambda i: (i // blocks_per_batch, 0, 5)),
            pl.BlockSpec((1, d), lambda i: (0, 0)),
        ],
        out_specs=pl.BlockSpec((tm, d), lambda i: (i, 0)),
        out_shape=jax.ShapeDtypeStruct((rows, d), F32),
        compiler_params=_params("arbitrary"),
        name="final",
    )(x1, peer_t, mod3, final_w)


def _rope_tables(t):
    rows = t // GRID_W
    row = jnp.repeat(jnp.arange(rows, dtype=F32), GRID_W)
    col = jnp.tile(jnp.arange(GRID_W, dtype=F32), rows)
    axis_dim = A_HALF_DIM // 2
    inv_freq = ROPE_BASE ** (-jnp.arange(0, axis_dim, 2, dtype=F32) / axis_dim)
    ang = jnp.concatenate([row[:, None] * inv_freq, col[:, None] * inv_freq], axis=-1)
    cos = jnp.tile(jnp.cos(ang), (1, LANES // axis_dim))
    sin = jnp.tile(jnp.concatenate([-jnp.sin(ang), jnp.sin(ang)], axis=-1), (1, LANES // (2 * axis_dim)))
    return cos, sin


def kernel(x, c, ctx, c_ctx, w_mod, b_mod, norm1_w, w_in, b_in, conv_w, conv_b, m_norm_w, lambdas, a_norm_w,
           w_pa, w_pb, w_out, norm2_w, w_pq, sub_keys, expert_u, expert_v, final_norm_w):
    depth = w_mod.shape[0]
    assert depth == 1, "single-layer configuration only (context outputs never reach a latent token)"
    bsz, t, d = x.shape
    t_ctx = ctx.shape[1]
    mw = M_HEADS * M_HEAD_DIM
    aw = A_HEADS * 2 * A_HALF_DIM
    assert w_in.shape[2] == 4 * mw + 4 * M_HEADS + 3 * aw + 2 * d and d == 2 * TILE_N and mw == TILE_N == aw
    lam_init = 0.8 - 0.6 * math.exp(-0.3 * 0)

    mod_rows = -(-(bsz + 1) // SUBLANES) * SUBLANES
    cvec = jnp.zeros((mod_rows, d), F32).at[:bsz].set(c).at[bsz].set(c_ctx)
    mod3 = _mod(cvec, w_mod[0], b_mod[0]).reshape(mod_rows, 1, 6 * d)

    wi, bi = w_in[0], b_in[0]
    offs = np.cumsum((0, mw, mw, mw, mw, 4 * M_HEADS, aw, aw, aw, d, d))
    seg = {n: slice(int(offs[i]), int(offs[i + 1]))
           for i, n in enumerate(("qm", "km", "vm", "zm", "g", "qa", "ka", "va", "ga", "gb"))}
    order = ("ga", "gb", "qm", "km", "zm", "vm", "qa", "ka", "va")
    w_main = jnp.concatenate([wi[:, seg[n]] for n in order], axis=1).astype(BF16)
    b_main = jnp.concatenate([bi[seg[n]] for n in order]).reshape(1, -1)
    gpad = LANES - 2 * M_HEADS
    wg, bg = wi[:, seg["g"]], bi[seg["g"]]
    w_g = jnp.concatenate([wg[:, :2 * M_HEADS], jnp.zeros((d, gpad), F32),
                           wg[:, 2 * M_HEADS:], jnp.zeros((d, gpad), F32)], axis=1).astype(BF16)
    b_g = jnp.concatenate([bg[:2 * M_HEADS], jnp.zeros((gpad,), F32),
                           bg[2 * M_HEADS:], jnp.zeros((gpad,), F32)]).reshape(1, -1)

    cos, sin = _rope_tables(t)
    cos_l = jnp.tile(cos, (bsz, 1))
    sin_l = jnp.tile(sin, (bsz, 1))
    n1 = norm1_w[0].reshape(1, d)

    tm_in = 1024
    blocks_per_batch = t // tm_in
    of_lat, ob_lat, og_lat = _inproj(x.reshape(bsz * t, d), mod3, lambda i: i // blocks_per_batch, n1,
                                     w_main, b_main, w_g, b_g, cos_l, sin_l, tm_in)
    rows_c = bsz * t_ctx
    of_ctx, ob_ctx, og_ctx = _inproj(ctx.reshape(rows_c, d), mod3, lambda i: bsz, n1, w_main, b_main, w_g, b_g,
                                     jnp.ones((rows_c, LANES), F32), jnp.zeros((rows_c, LANES), F32), rows_c)

    qk_lat = _conv(of_lat.reshape(bsz, t, -1), conv_w[0], conv_b[0], 512)
    qk_ctx = _conv(of_ctx.reshape(bsz, t_ctx, -1), conv_w[0], conv_b[0], t_ctx)
    ob_lat3 = ob_lat.reshape(bsz, t, -1)
    ob_ctx3 = ob_ctx.reshape(bsz, t_ctx, -1)
    hdir = _mlstm(qk_ctx, ob_ctx3, og_ctx.reshape(bsz, t_ctx, -1), qk_lat, ob_lat3, og_lat.reshape(bsz, t, -1))

    lq1, lk1, lq2, lk2 = lambdas[0].astype(F32)
    lam = (jnp.exp(jnp.sum(lq1 * lk1)) - jnp.exp(jnp.sum(lq2 * lk2)) + lam_init).reshape(1, 1)
    o_attn = _attn(lam, ob_lat3, ob_ctx3, ATTN_TQ, min(ATTN_TK, t), ATTN_RC)

    merged = _merge(hdir.reshape(2, bsz * t, mw), of_lat, o_attn.reshape(bsz * t, aw),
                    m_norm_w[0].reshape(1, mw), jnp.tile(a_norm_w[0], A_HEADS).reshape(1, aw),
                    w_pa[0].astype(BF16), w_pb[0].astype(BF16), 1.0 - lam_init, 256)
    tm_r = 256
    x1, h2 = _resid(merged, w_out[0].astype(BF16), x.reshape(bsz * t, d), mod3, norm2_w[0].reshape(1, d),
                    t // tm_r, tm_r)

    nk = sub_keys.shape[3]
    keys = sub_keys[0].reshape(2 * P_HEADS, nk, sub_keys.shape[4]).astype(BF16)
    rank2, e2, rc, e1 = _psel(h2, w_pq[0].astype(BF16), keys, 256)
    tm_p = 512
    peer_t = _peer(h2, expert_u[0].astype(BF16), expert_v[0].T.astype(BF16), rank2, e2, rc, e1, tm_p, PEER_TE, PEER_TC)

    tm_f = 256
    out = _final(x1, peer_t, mod3, final_norm_w.reshape(1, d), t // tm_f, tm_f)
    return out.reshape(bsz, t, d)
```
